```python
import math
import jax
import jax.numpy as jnp
from jax import lax
import numpy as np

D_MODEL = 1024
BATCH = 16
SEQ = 256
DEPTH = 4
DEC_BATCH = 8
DEC_SEQ = 2048
PAST_LEN = 256

GRID_W = 64
ROPE_THETA = 10000.0
EPS = 1e-6
Q_BLOCK = 128
H_A = 8
NOPE_A = 64
ROPE_A = 32
V_A = 64
Q_RANK = 384
KV_RANK = 256
H_B = 16
P_B = 64
D_INNER = H_B * P_B
N_STATE = 128
G_B = 4
SSD_CONV = 3
CHUNK = 128
CONV_CH = D_INNER + 2 * G_B * N_STATE
H_C = 4
DH_C = 64
DV_C = 2 * DH_C
D_FF = 2816
FFN_CONV = 3
N_BRANCH = 3
SPLIT_SIZES = (Q_RANK, KV_RANK, ROPE_A, D_INNER, CONV_CH, 2 * H_B, H_C * 2 * DH_C, H_C * 2 * DH_C, H_C * DV_C, N_BRANCH * D_MODEL)
D_IN_PROJ = sum(SPLIT_SIZES)

kernel_name = 'hybrid_mla_ssd_diffattn_prefix_dit_step'


def rms_norm(x, g):
    xf = x.astype(jnp.float32)
    y = xf * lax.rsqrt(jnp.mean(xf * xf, axis=-1, keepdims=True) + EPS)
    return (y * g.astype(jnp.float32)).astype(x.dtype)


def axial_rope_tables(n_tokens, rot_dim):
    rows = n_tokens // GRID_W
    row = jnp.broadcast_to(jnp.arange(rows, dtype=jnp.float32)[:, None], (rows, GRID_W)).reshape(-1)
    col = jnp.broadcast_to(jnp.arange(GRID_W, dtype=jnp.float32)[None, :], (rows, GRID_W)).reshape(-1)
    n_freq = rot_dim // 4
    freqs = ROPE_THETA ** (-jnp.arange(n_freq, dtype=jnp.float32) / n_freq)
    ang = jnp.concatenate([row[:, None] * freqs, col[:, None] * freqs], axis=-1)
    return jnp.cos(ang), jnp.sin(ang)


def apply_rope(x, cos, sin):
    shape = x.shape
    xp = x.reshape(shape[:-1] + (shape[-1] // 2, 2))
    x0, x1 = xp[..., 0], xp[..., 1]
    bshape = (shape[1],) + (1,) * (x.ndim - 3) + (shape[-1] // 2,)
    c = cos.reshape(bshape).astype(x.dtype)
    s = sin.reshape(bshape).astype(x.dtype)
    return jnp.stack([x0 * c - x1 * s, x0 * s + x1 * c], axis=-1).reshape(shape)


def attend(q, k, v, scale):
    b, lq, h, m, dk = q.shape
    dv = v.shape[-1]
    nb = lq // Q_BLOCK
    qb = q.reshape(b, nb, Q_BLOCK, h, m, dk).swapaxes(0, 1)

    def one_block(qblk):
        s = jnp.einsum('bqhmd,bkhmd->bhmqk', qblk, k).astype(jnp.float32) * scale
        p = jax.nn.softmax(s, axis=-1).astype(v.dtype)
        return jnp.einsum('bhmqk,bkhv->bqhmv', p, v)

    o = lax.map(one_block, qb)
    return o.swapaxes(0, 1).reshape(b, lq, h, m, dv)


def dwconv(x, w, bias):
    ch = x.shape[-1]
    y = lax.conv_general_dilated(x, w[:, None, :].astype(x.dtype), window_strides=(1,), padding='SAME',
                                 dimension_numbers=('NWC', 'WIO', 'NWC'), feature_group_count=ch)
    return y + bias.astype(x.dtype)


def ssd_scan(x, dt, a, bm, cm, h0):
    b, L, H, P = x.shape
    G, N = bm.shape[2], bm.shape[3]
    hg = H // G
    nc = L // CHUNK
    xr = x.reshape(b, nc, CHUNK, G, hg, P)
    dtr = dt.reshape(b, nc, CHUNK, G, hg)
    br = bm.reshape(b, nc, CHUNK, G, N)
    cr = cm.reshape(b, nc, CHUNK, G, N)
    la = (dtr * a.reshape(G, hg)).astype(jnp.float32)
    acs = jnp.cumsum(la, axis=2)
    seg = acs[:, :, :, None] - acs[:, :, None, :]
    mask = jnp.tril(jnp.ones((CHUNK, CHUNK), dtype=bool))[:, :, None, None]
    lmat = jnp.exp(jnp.where(mask, seg, -jnp.inf))
    xdt = xr * dtr[..., None]
    cb = jnp.einsum('bcigN,bcjgN->bcijg', cr, br)
    y_diag = jnp.einsum('bcijgh,bcjghp->bcighp', cb[..., None] * lmat, xdt)
    decay_end = jnp.exp(acs[:, :, -1:] - acs)
    states = jnp.einsum('bcjgN,bcjghp->bcghpN', br, xdt * decay_end[..., None]).astype(jnp.float32)
    chunk_decay = jnp.exp(acs[:, :, -1])

    def step(hc, inp):
        s, d = inp
        return hc * d[..., None, None] + s, hc

    h_final, h_prev = lax.scan(step, h0.reshape(b, G, hg, P, N).astype(jnp.float32),
                               (states.swapaxes(0, 1), chunk_decay.swapaxes(0, 1)))
    h_prev = h_prev.swapaxes(0, 1)
    y_off = jnp.einsum('bcigN,bcghpN->bcighp', cr, h_prev) * jnp.exp(acs)[..., None]
    y = (y_diag + y_off).reshape(b, L, H, P).astype(x.dtype)
    return y, h_final.reshape(b, H, P, N).astype(x.dtype)


def token_mixers(h, lp, lam_init, rope, ctx):
    b, L, _ = h.shape
    split_points = [int(v) for v in np.cumsum(SPLIT_SIZES)[:-1]]
    cq, ckv, krope, z, xbc, dt_raw, qc, kc, vc, gate_raw = jnp.split(h @ lp['w_in'], split_points, axis=-1)
    cq = rms_norm(cq, lp['q_norm_g'])
    qa = (cq @ lp['w_uq']).reshape(b, L, H_A, NOPE_A + ROPE_A)
    qa_nope, qa_rope = qa[..., :NOPE_A], qa[..., NOPE_A:]
    ckv = rms_norm(ckv, lp['kv_norm_g'])
    qc = qc.reshape(b, L, H_C, 2, DH_C)
    kc = kc.reshape(b, L, H_C, 2, DH_C)
    vc = vc.reshape(b, L, H_C, DV_C)
    if ctx is None:
        keys_ckv, keys_krope, keys_kc, keys_vc = ckv, krope, kc, vc
        h0 = jnp.zeros((b, 2, H_B, P_B, N_STATE), h.dtype)
    else:
        (cos_a, sin_a), (cos_c, sin_c) = rope
        ctx_ckv, ctx_krope, ctx_k, ctx_v, h0 = ctx
        qa_rope = apply_rope(qa_rope, cos_a, sin_a)
        qc = apply_rope(qc, cos_c, sin_c)
        keys_ckv = jnp.concatenate([ctx_ckv, ckv], axis=1)
        keys_krope = jnp.concatenate([ctx_krope, apply_rope(krope, cos_a, sin_a)], axis=1)
        keys_kc = jnp.concatenate([ctx_k, apply_rope(kc, cos_c, sin_c)], axis=1)
        keys_vc = jnp.concatenate([ctx_v, vc], axis=1)
    S = keys_ckv.shape[1]
    kv = (keys_ckv @ lp['w_ukv']).reshape(b, S, H_A, NOPE_A + V_A)
    k_nope, va = kv[..., :NOPE_A], kv[..., NOPE_A:]
    ka = jnp.concatenate([k_nope, jnp.broadcast_to(keys_krope[:, :, None, :], (b, S, H_A, ROPE_A))], axis=-1)
    qa_full = jnp.concatenate([qa_nope, qa_rope], axis=-1)
    oa = attend(qa_full[:, :, :, None], ka[:, :, :, None], va, (NOPE_A + ROPE_A) ** -0.5)[:, :, :, 0]
    out_a = oa.reshape(b, L, H_A * V_A) @ lp['w_oA']
    oc = attend(qc, keys_kc, keys_vc, DH_C ** -0.5)
    lv = lp['diff_lambda'].astype(jnp.float32)
    lam = jnp.exp(jnp.sum(lv[0] * lv[1])) - jnp.exp(jnp.sum(lv[2] * lv[3])) + lam_init
    oc = oc[..., 0, :] - lam.astype(oc.dtype) * oc[..., 1, :]
    oc = rms_norm(oc, lp['diff_norm_g']) * (1.0 - lam_init)
    out_c = oc.reshape(b, L, H_C * DV_C) @ lp['w_oC']
    xbc = jax.nn.silu(dwconv(xbc, lp['ssd_conv_w'], lp['ssd_conv_b']))
    xs, bs, cs = jnp.split(xbc, [D_INNER, D_INNER + G_B * N_STATE], axis=-1)
    xs = xs.reshape(b, L, H_B, P_B)
    bs = bs.reshape(b, L, G_B, N_STATE)
    cs = cs.reshape(b, L, G_B, N_STATE)
    dt = jax.nn.softplus(dt_raw.reshape(b, L, 2, H_B) + lp['ssd_dt_bias'])
    a = -jnp.exp(lp['ssd_a_log'])
    y_f, h_f = ssd_scan(xs, dt[:, :, 0], a[0], bs, cs, h0[:, 0])
    y_b, h_b = ssd_scan(jnp.flip(xs, 1), jnp.flip(dt[:, :, 1], 1), a[1], jnp.flip(bs, 1), jnp.flip(cs, 1), h0[:, 1])
    y = y_f + jnp.flip(y_b, 1) + lp['ssd_d_skip'][:, None] * xs
    y = rms_norm(y.reshape(b, L, D_INNER) * jax.nn.silu(z), lp['ssd_norm_g'])
    out_b = y @ lp['w_oB']
    g = jax.nn.sigmoid(gate_raw).reshape(b, L, N_BRANCH, D_MODEL)
    merged = g[:, :, 0] * out_a + g[:, :, 1] * out_b + g[:, :, 2] * out_c
    out = merged @ lp['w_out']
    if ctx is None:
        return out, (ckv, krope, kc, vc, jnp.stack([h_f, h_b], axis=1))
    return out, None


def conv_ffn(h, w_up, conv_w, conv_b, w_down):
    u = dwconv(h @ w_up, conv_w, conv_b)
    gt, val = jnp.split(u, 2, axis=-1)
    return (jax.nn.silu(gt) * val) @ w_down


def block(x, cond, lp, lam_init, rope, ctx):
    mod = jax.nn.silu(cond) @ lp['w_mod'] + lp['b_mod']
    sh1, sc1, g1, sh2, sc2, g2 = jnp.split(mod[:, None, :], 6, axis=-1)
    h = rms_norm(x, lp['norm1_g']) * (1.0 + sc1) + sh1
    mix, st = token_mixers(h, lp, lam_init, rope, ctx)
    x = x + g1 * mix
    h = rms_norm(x, lp['norm2_g']) * (1.0 + sc2) + sh2
    x = x + g2 * conv_ffn(h, lp['ffn_w_up'], lp['ffn_conv_w'], lp['ffn_conv_b'], lp['ffn_w_down'])
    return x, st


def setup_inputs(seed: int = 0) -> dict:
    key = jax.random.key(seed)
    ks = iter(jax.random.split(key, 48))

    def nrm(shape, scale=1.0):
        return jax.random.normal(next(ks), shape, jnp.float32) * scale

    def gain(shape):
        return 1.0 + nrm(shape, 0.01)

    L = DEPTH
    dt0 = jnp.exp(jax.random.uniform(next(ks), (L, 2, H_B), jnp.float32, math.log(1e-3), math.log(1e-1)))
    dt_bias = dt0 + jnp.log(-jnp.expm1(-dt0))
    a_log = jnp.log(jax.random.uniform(next(ks), (L, 2, H_B), jnp.float32, 1.0, 16.0))
    return {
        'x_prompt': nrm((BATCH, SEQ, D_MODEL)),
        'x_sample': nrm((DEC_BATCH, DEC_SEQ, D_MODEL)),
        'cache_mla_ckv': nrm((DEC_BATCH, DEPTH, PAST_LEN, KV_RANK)),
        'cache_mla_krope': nrm((DEC_BATCH, DEPTH, PAST_LEN, ROPE_A)),
        'cache_diff_k': nrm((DEC_BATCH, DEPTH, PAST_LEN, H_C, 2, DH_C)),
        'cache_diff_v': nrm((DEC_BATCH, DEPTH, PAST_LEN, H_C, DV_C)),
        'state_ssd': nrm((DEC_BATCH, DEPTH, 2, H_B, P_B, N_STATE), 0.1),
        'c': nrm((DEC_BATCH, D_MODEL)),
        'c_ctx': nrm((D_MODEL,)),
        'w_mod': nrm((L, D_MODEL, 6 * D_MODEL), 0.5 * D_MODEL ** -0.5),
        'b_mod': nrm((L, 6 * D_MODEL), 0.01),
        'norm1_g': gain((L, D_MODEL)),
        'norm2_g': gain((L, D_MODEL)),
        'w_in': nrm((L, D_MODEL, D_IN_PROJ), D_MODEL ** -0.5),
        'q_norm_g': gain((L, Q_RANK)),
        'kv_norm_g': gain((L, KV_RANK)),
        'w_uq': nrm((L, Q_RANK, H_A * (NOPE_A + ROPE_A)), Q_RANK ** -0.5),
        'w_ukv': nrm((L, KV_RANK, H_A * (NOPE_A + V_A)), KV_RANK ** -0.5),
        'w_oA': nrm((L, H_A * V_A, D_MODEL), (H_A * V_A) ** -0.5),
        'ssd_conv_w': nrm((L, SSD_CONV, CONV_CH), SSD_CONV ** -0.5),
        'ssd_conv_b': nrm((L, CONV_CH), 0.01),
        'ssd_dt_bias': dt_bias,
        'ssd_a_log': a_log,
        'ssd_d_skip': gain((L, H_B)),
        'ssd_norm_g': gain((L, D_INNER)),
        'w_oB': nrm((L, D_INNER, D_MODEL), D_INNER ** -0.5),
        'diff_lambda': nrm((L, 4, DH_C), 0.1),
        'diff_norm_g': gain((L, DV_C)),
        'w_oC': nrm((L, H_C * DV_C, D_MODEL), (H_C * DV_C) ** -0.5),
        'w_out': nrm((L, D_MODEL, D_MODEL), D_MODEL ** -0.5),
        'ffn_w_up': nrm((L, D_MODEL, 2 * D_FF), D_MODEL ** -0.5),
        'ffn_conv_w': nrm((L, FFN_CONV, 2 * D_FF), FFN_CONV ** -0.5),
        'ffn_conv_b': nrm((L, 2 * D_FF), 0.01),
        'ffn_w_down': nrm((L, D_FF, D_MODEL), D_FF ** -0.5),
        'final_norm_g': gain((D_MODEL,)),
    }


def reference(x_prompt, x_sample, cache_mla_ckv, cache_mla_krope, cache_diff_k, cache_diff_v, state_ssd,
              c, c_ctx, w_mod, b_mod, norm1_g, norm2_g, w_in, q_norm_g, kv_norm_g, w_uq, w_ukv, w_oA,
              ssd_conv_w, ssd_conv_b, ssd_dt_bias, ssd_a_log, ssd_d_skip, ssd_norm_g, w_oB,
              diff_lambda, diff_norm_g, w_oC, w_out, ffn_w_up, ffn_conv_w, ffn_conv_b, ffn_w_down,
              final_norm_g):
    n_lat = x_sample.shape[1]
    rope = (axial_rope_tables(n_lat, ROPE_A), axial_rope_tables(n_lat, DH_C))
    xp, xs = x_prompt, x_sample
    new_ckv, new_krope, new_dk, new_dv, new_ssd = [], [], [], [], []
    for l in range(DEPTH):
        lp = {
            'w_mod': w_mod[l], 'b_mod': b_mod[l], 'norm1_g': norm1_g[l], 'norm2_g': norm2_g[l],
            'w_in': w_in[l], 'q_norm_g': q_norm_g[l], 'kv_norm_g': kv_norm_g[l], 'w_uq': w_uq[l],
            'w_ukv': w_ukv[l], 'w_oA': w_oA[l], 'ssd_conv_w': ssd_conv_w[l], 'ssd_conv_b': ssd_conv_b[l],
            'ssd_dt_bias': ssd_dt_bias[l], 'ssd_a_log': ssd_a_log[l], 'ssd_d_skip': ssd_d_skip[l],
            'ssd_norm_g': ssd_norm_g[l], 'w_oB': w_oB[l], 'diff_lambda': diff_lambda[l],
            'diff_norm_g': diff_norm_g[l], 'w_oC': w_oC[l], 'w_out': w_out[l], 'ffn_w_up': ffn_w_up[l],
            'ffn_conv_w': ffn_conv_w[l], 'ffn_conv_b': ffn_conv_b[l], 'ffn_w_down': ffn_w_down[l],
        }
        lam_init = 0.8 - 0.6 * math.exp(-0.3 * l)
        xp, st = block(xp, c_ctx[None, :], lp, lam_init, None, None)
        new_ckv.append(st[0])
        new_krope.append(st[1])
        new_dk.append(st[2])
        new_dv.append(st[3])
        new_ssd.append(st[4])
        ctx = (cache_mla_ckv[:, l], cache_mla_krope[:, l], cache_diff_k[:, l], cache_diff_v[:, l], state_ssd[:, l])
        xs, _ = block(xs, c, lp, lam_init, rope, ctx)
    y_prompt = rms_norm(xp, final_norm_g)
    y_sample = rms_norm(xs, final_norm_g)
    new_mla_ckv = jnp.stack(new_ckv, axis=1)
    new_mla_krope = jnp.stack(new_krope, axis=1)
    new_diff_k = jnp.stack(new_dk, axis=1)
    new_diff_v = jnp.stack(new_dv, axis=1)
    new_state_ssd = jnp.stack(new_ssd, axis=1)
    return (y_prompt, y_sample, new_mla_ckv, new_mla_krope, new_diff_k, new_diff_v, new_state_ssd)
```

```python
import functools
import math

import jax
import jax.numpy as jnp
from jax import lax
from jax.experimental import pallas as pl
from jax.experimental.pallas import tpu as pltpu

F32 = jnp.float32
BF16 = jnp.bfloat16

D_MODEL = 1024
DEPTH = 4
GRID_W = 64
ROPE_THETA = 10000.0
EPS = 1e-6
H_A, NOPE_A, ROPE_A, V_A = 8, 64, 32, 64
Q_RANK, KV_RANK = 384, 256
H_B, P_B, N_STATE, G_B, CHUNK = 16, 64, 128, 4, 128
D_INNER = H_B * P_B
CONV_CH = D_INNER + 2 * G_B * N_STATE
H_C, DH_C = 4, 64
DV_C = 2 * DH_C
D_FF = 2816
N_MOD = 6
COND_ROWS = 16

LANE = 128
SUBLANE = 8
VMEM_LIMIT = 56 * 1024 * 1024

C_GATE = 0
C_Z = 3072
C_XBC = 4096
C_QC = 6144
C_KC = 7168
C_VC = 7680
C_CQ = 8192
C_KROPE = 8576
C_CKV = 8704
C_DT = 8960
N_PROJ = 9216


def _cparams(*sem):
    return pltpu.CompilerParams(dimension_semantics=sem, vmem_limit_bytes=VMEM_LIMIT)


def _silu(x):
    return x * jax.nn.sigmoid(x)


def _rms(x, g):
    ms = jnp.mean(x * x, axis=-1, keepdims=True)
    return x * lax.rsqrt(ms + EPS) * g


def _rope(x, c, s):
    n = x.shape[-1]
    lane = lax.broadcasted_iota(jnp.int32, x.shape, x.ndim - 1)
    nxt = pltpu.roll(x, n - 1, x.ndim - 1)
    prv = pltpu.roll(x, 1, x.ndim - 1)
    return x * c + jnp.where(lane % 2 == 0, nxt, prv) * s


def _mod_kernel(c_ref, w_ref, b_ref, o_ref):
    s = _silu(c_ref[...]).astype(BF16)
    o_ref[0] = jnp.dot(s, w_ref[0].astype(BF16), preferred_element_type=F32) + b_ref[0]


def _mod_call(cond, w_mod, b_mod):
    tn = 1536
    n = N_MOD * D_MODEL
    return pl.pallas_call(
        _mod_kernel,
        grid=(DEPTH, n // tn),
        in_specs=[
            pl.BlockSpec((COND_ROWS, D_MODEL), lambda l, j: (0, 0)),
            pl.BlockSpec((1, D_MODEL, tn), lambda l, j: (l, 0, j)),
            pl.BlockSpec((1, 1, tn), lambda l, j: (l, 0, j)),
        ],
        out_specs=pl.BlockSpec((1, COND_ROWS, tn), lambda l, j: (l, 0, j)),
        out_shape=jax.ShapeDtypeStruct((DEPTH, COND_ROWS, n), F32),
        compiler_params=_cparams("parallel", "parallel"),
        name="mod",
    )(cond, w_mod, b_mod.reshape(DEPTH, 1, n))


def _inproj_kernel(x_ref, g_ref, mod_ref, w_ref, o_ref, h_scr):
    @pl.when(pl.program_id(1) == 0)
    def _():
        h = _rms(x_ref[...], g_ref[...]) * (1.0 + mod_ref[0, 1:2, :]) + mod_ref[0, 0:1, :]
        h_scr[...] = h.astype(BF16)

    o_ref[...] = jnp.dot(h_scr[...], w_ref[...], preferred_element_type=F32)


def _inproj_call(x2d, g, mod, w, row_fn, tm):
    t = x2d.shape[0]
    tn = 512
    return pl.pallas_call(
        _inproj_kernel,
        grid=(t // tm, N_PROJ // tn),
        in_specs=[
            pl.BlockSpec((tm, D_MODEL), lambda i, j: (i, 0)),
            pl.BlockSpec((1, D_MODEL), lambda i, j: (0, 0)),
            pl.BlockSpec((1, N_MOD, D_MODEL), lambda i, j: (row_fn(i, tm), 0, 0)),
            pl.BlockSpec((D_MODEL, tn), lambda i, j: (0, j)),
        ],
        out_specs=pl.BlockSpec((tm, tn), lambda i, j: (i, j)),
        out_shape=jax.ShapeDtypeStruct((t, N_PROJ), F32),
        scratch_shapes=[pltpu.VMEM((tm, D_MODEL), BF16)],
        compiler_params=_cparams("parallel", "arbitrary"),
        name="inproj",
    )(x2d, g, mod, w)


def _mla_q_kernel(*refs, rope, scale):
    if rope:
        p_ref, g_ref, w_ref, c_ref, s_ref, o_ref = refs
    else:
        p_ref, g_ref, w_ref, o_ref = refs
    cqn = _rms(p_ref[:, :Q_RANK], g_ref[...]).astype(BF16)
    q = jnp.dot(cqn, w_ref[...], preferred_element_type=F32)
    for h in range(H_A):
        qh = q[:, h * LANE:(h + 1) * LANE]
        if rope:
            qh = _rope(qh, c_ref[...], s_ref[...])
        o_ref[:, h * LANE:(h + 1) * LANE] = (qh * scale).astype(BF16)


def _mla_q_call(proj, g, w, tabs, seq, tm):
    t = proj.shape[0]
    rope = tabs is not None
    nb = seq // tm
    in_specs = [
        pl.BlockSpec((tm, 512), lambda i: (i, C_CQ // 512)),
        pl.BlockSpec((1, Q_RANK), lambda i: (0, 0)),
        pl.BlockSpec((Q_RANK, H_A * LANE), lambda i: (0, 0)),
    ]
    args = [proj, g, w]
    if rope:
        in_specs += [pl.BlockSpec((tm, LANE), lambda i: (i % nb, 0))] * 2
        args += list(tabs)
    return pl.pallas_call(
        functools.partial(_mla_q_kernel, rope=rope, scale=(NOPE_A + ROPE_A) ** -0.5),
        grid=(t // tm,),
        in_specs=in_specs,
        out_specs=pl.BlockSpec((tm, H_A * LANE), lambda i: (i, 0)),
        out_shape=jax.ShapeDtypeStruct((t, H_A * LANE), BF16),
        compiler_params=_cparams("parallel"),
        name="mla_q",
    )(*args)


def _kv_up_kernel(*refs, norm, rope):
    refs = list(refs)
    ckv_ref, kr_ref = refs[0], refs[1]
    pos = 2
    if norm:
        g_ref = refs[pos]
        pos += 1
    w_ref = refs[pos]
    pos += 1
    if rope:
        c_ref, s_ref = refs[pos], refs[pos + 1]
        pos += 2
    k_ref, v_ref = refs[pos], refs[pos + 1]
    ckv = ckv_ref[...]
    if norm:
        ckv = _rms(ckv, g_ref[...])
        refs[pos + 2][...] = ckv
    kv = jnp.dot(ckv.astype(BF16), w_ref[...], preferred_element_type=F32)
    kr = kr_ref[...]
    if rope:
        kr = _rope(kr, c_ref[...], s_ref[...])
    for h in range(H_A):
        k_ref[:, h * LANE:(h + 1) * LANE] = (kv[:, h * LANE:(h + 1) * LANE] + kr).astype(BF16)
    v_ref[...] = kv[:, H_A * LANE:].astype(BF16)


def _kv_up_call(ckv_src, ckv_col, kr_src, kr_col, g, w, tabs, seq, tm):
    t = ckv_src.shape[0]
    norm = g is not None
    rope = tabs is not None
    nb = seq // tm
    in_specs = [
        pl.BlockSpec((tm, KV_RANK), lambda i: (i, ckv_col)),
        pl.BlockSpec((tm, LANE), lambda i: (i, kr_col)),
    ]
    args = [ckv_src, kr_src]
    if norm:
        in_specs.append(pl.BlockSpec((1, KV_RANK), lambda i: (0, 0)))
        args.append(g)
    in_specs.append(pl.BlockSpec((KV_RANK, 2 * H_A * LANE), lambda i: (0, 0)))
    args.append(w)
    if rope:
        in_specs += [pl.BlockSpec((tm, LANE), lambda i: (i % nb, 0))] * 2
        args += list(tabs)
    out_specs = [pl.BlockSpec((tm, H_A * LANE), lambda i: (i, 0))] * 2
    out_shape = [jax.ShapeDtypeStruct((t, H_A * LANE), BF16)] * 2
    if norm:
        out_specs.append(pl.BlockSpec((tm, KV_RANK), lambda i: (i, 0)))
        out_shape.append(jax.ShapeDtypeStruct((t, KV_RANK), F32))
    return pl.pallas_call(
        functools.partial(_kv_up_kernel, norm=norm, rope=rope),
        grid=(t // tm,),
        in_specs=in_specs,
        out_specs=out_specs,
        out_shape=out_shape,
        compiler_params=_cparams("parallel"),
        name="kv_up",
    )(*args)


def _dprep_kernel(*refs, rope):
    if rope:
        q_ref, k_ref, v_ref, c_ref, s_ref, qo_ref, ko_ref, vo_ref = refs
    else:
        q_ref, k_ref, v_ref, qo_ref, ko_ref, vo_ref = refs
    for b in range(2 * H_C):
        q = q_ref[:, b * LANE:(b + 1) * LANE]
        if rope:
            q = _rope(q, c_ref[...], s_ref[...])
        qo_ref[:, b * LANE:(b + 1) * LANE] = (q * (DH_C ** -0.5)).astype(BF16)
    for b in range(H_C):
        k = k_ref[:, b * LANE:(b + 1) * LANE]
        if rope:
            k = _rope(k, c_ref[...], s_ref[...])
        ko_ref[:, b * LANE:(b + 1) * LANE] = k.astype(BF16)
    vo_ref[...] = v_ref[...].astype(BF16)


def _dprep_call(proj, tabs, seq, tm):
    t = proj.shape[0]
    rope = tabs is not None
    nb = seq // tm
    in_specs = [
        pl.BlockSpec((tm, 1024), lambda i: (i, C_QC // 1024)),
        pl.BlockSpec((tm, 512), lambda i: (i, C_KC // 512)),
        pl.BlockSpec((tm, 512), lambda i: (i, C_VC // 512)),
    ]
    args = [proj, proj, proj]
    if rope:
        in_specs += [pl.BlockSpec((tm, LANE), lambda i: (i % nb, 0))] * 2
        args += list(tabs)
    return pl.pallas_call(
        functools.partial(_dprep_kernel, rope=rope),
        grid=(t // tm,),
        in_specs=in_specs,
        out_specs=[
            pl.BlockSpec((tm, 1024), lambda i: (i, 0)),
            pl.BlockSpec((tm, 512), lambda i: (i, 0)),
            pl.BlockSpec((tm, 512), lambda i: (i, 0)),
        ],
        out_shape=[
            jax.ShapeDtypeStruct((t, 1024), BF16),
            jax.ShapeDtypeStruct((t, 512), BF16),
            jax.ShapeDtypeStruct((t, 512), BF16),
        ],
        compiler_params=_cparams("parallel"),
        name="dprep",
    )(*args)


def _attn_kernel(*refs, n_pieces, diff, tq, lam_init):
    q_ref = refs[0]
    kv = refs[1:1 + 2 * n_pieces]
    o_ref = refs[-1]
    q = q_ref[0]
    if diff:
        q = jnp.concatenate([q[:, :LANE], q[:, LANE:]], axis=0)
    nt = (((1,), (1,)), ((), ()))
    ss = [lax.dot_general(q, kv[2 * i][0].astype(BF16), nt, preferred_element_type=F32)
          for i in range(n_pieces)]
    m = ss[0].max(axis=-1, keepdims=True)
    for s in ss[1:]:
        m = jnp.maximum(m, s.max(axis=-1, keepdims=True))
    l = None
    o = None
    for i in range(n_pieces):
        p = jnp.exp(ss[i] - m)
        li = p.sum(axis=-1, keepdims=True)
        oi = jnp.dot(p.astype(BF16), kv[2 * i + 1][0].astype(BF16), preferred_element_type=F32)
        l = li if l is None else l + li
        o = oi if o is None else o + oi
    o = o / l
    if diff:
        lam_ref, g_ref = refs[1 + 2 * n_pieces], refs[2 + 2 * n_pieces]
        lv = lam_ref[...]
        lam = (jnp.exp(jnp.sum(lv[0:1] * lv[1:2], axis=-1, keepdims=True))
               - jnp.exp(jnp.sum(lv[2:3] * lv[3:4], axis=-1, keepdims=True)) + lam_init)
        oc = o[:tq] - lam * o[tq:]
        o = _rms(oc, g_ref[...]) * (1.0 - lam_init)
    o_ref[0] = o.astype(BF16)


def _attn_call(q, pieces, n_heads, diff, tq, extra=(), lam_init=0.0):
    b, seq = q.shape[0], q.shape[1]
    qw = 2 * LANE if diff else LANE
    in_specs = [pl.BlockSpec((1, tq, qw), lambda bi, h, qi: (bi, qi, h))]
    args = [q]
    for k, v in pieces:
        s = k.shape[1]
        in_specs += [pl.BlockSpec((1, s, LANE), lambda bi, h, qi: (bi, 0, h))] * 2
        args += [k, v]
    for e in extra:
        in_specs.append(pl.BlockSpec(e.shape, lambda bi, h, qi: (0, 0)))
        args.append(e)
    return pl.pallas_call(
        functools.partial(_attn_kernel, n_pieces=len(pieces), diff=diff, tq=tq, lam_init=lam_init),
        grid=(b, n_heads, seq // tq),
        in_specs=in_specs,
        out_specs=pl.BlockSpec((1, tq, LANE), lambda bi, h, qi: (bi, qi, h)),
        out_shape=jax.ShapeDtypeStruct((b, seq, n_heads * LANE), BF16),
        compiler_params=_cparams("parallel", "parallel", "arbitrary"),
        name="diff_attn" if diff else "mla_attn",
    )(*args)


def _conv3(x, prev_ref, next_ref, w_ref, b_ref, tm, seq):
    i = pl.program_id(0)
    first = (i * tm) % seq == 0
    last = ((i + 1) * tm) % seq == 0
    pv = prev_ref[SUBLANE - 1:SUBLANE, :] * jnp.where(first, 0.0, 1.0)
    nx = next_ref[0:1, :] * jnp.where(last, 0.0, 1.0)
    row = lax.broadcasted_iota(jnp.int32, x.shape, 0)
    xp = jnp.where(row == 0, pv, pltpu.roll(x, 1, 0))
    xn = jnp.where(row == tm - 1, nx, pltpu.roll(x, tm - 1, 0))
    return w_ref[0:1, :] * xp + w_ref[1:2, :] * x + w_ref[2:3, :] * xn + b_ref[...]


def _halo_specs(tm, width, col, t):
    r = tm // SUBLANE
    nblk = t // SUBLANE
    return [
        pl.BlockSpec((tm, width), lambda i: (i, col)),
        pl.BlockSpec((SUBLANE, width), lambda i: (jnp.maximum(i * r - 1, 0), col)),
        pl.BlockSpec((SUBLANE, width), lambda i: (jnp.minimum((i + 1) * r, nblk - 1), col)),
    ]


def _ssd_conv_kernel(x_ref, prev_ref, next_ref, w_ref, b_ref, o_ref, *, tm, seq):
    o_ref[...] = _silu(_conv3(x_ref[...], prev_ref, next_ref, w_ref, b_ref, tm, seq))


def _ssd_conv_call(proj, w, b, seq, tm):
    t = proj.shape[0]
    return pl.pallas_call(
        functools.partial(_ssd_conv_kernel, tm=tm, seq=seq),
        grid=(t // tm,),
        in_specs=_halo_specs(tm, CONV_CH, C_XBC // CONV_CH, t) + [
            pl.BlockSpec((3, CONV_CH), lambda i: (0, 0)),
            pl.BlockSpec((1, CONV_CH), lambda i: (0, 0)),
        ],
        out_specs=pl.BlockSpec((tm, CONV_CH), lambda i: (i, 0)),
        out_shape=jax.ShapeDtypeStruct((t, CONV_CH), F32),
        compiler_params=_cparams("parallel"),
        name="ssd_conv",
    )(proj, proj, proj, w, b)


def _cumsum_rows(x, reverse):
    n = x.shape[0]
    row = lax.broadcasted_iota(jnp.int32, x.shape, 0)
    s = 1
    while s < n:
        if reverse:
            x = x + jnp.where(row < n - s, pltpu.roll(x, n - s, 0), 0.0)
        else:
            x = x + jnp.where(row >= s, pltpu.roll(x, s, 0), 0.0)
        s *= 2
    return x


def _pair_cols(v, lane, h):
    return jnp.where(lane < P_B, v[:, h:h + 1], v[:, h + 1:h + 2])


def _ssd_kernel(*refs, nc, has_h0):
    if has_h0:
        xs_ref, bs_ref, cs_ref, dt_ref, dtb_ref, alog_ref, dsk_ref, h0_ref = refs[:8]
        y_ref, hfin_ref, h_scr, hb_scr = refs[8:]
    else:
        xs_ref, bs_ref, cs_ref, dt_ref, dtb_ref, alog_ref, dsk_ref = refs[:7]
        y_ref, hfin_ref, h_scr, hb_scr = refs[7:]
    ph = pl.program_id(1)
    k = pl.program_id(2)
    q = CHUNK
    hpg = H_B // G_B
    gw = hpg * P_B
    dtp = jax.nn.softplus(dt_ref[0] + dtb_ref[...])
    la = dtp * (-jnp.exp(alog_ref[...]))
    lane = lax.broadcasted_iota(jnp.int32, (q, LANE), 1)
    row = lax.broadcasted_iota(jnp.int32, (q, LANE), 0)
    tn_dims = (((0,), (0,)), ((), ()))
    nt_dims = (((1,), (1,)), ((), ()))

    def init_state(d):
        if has_h0:
            h_scr[...] = h0_ref[0, d]
        else:
            h_scr[...] = jnp.zeros_like(h_scr)

    def update_state(w, tot, off):
        for g in range(G_B):
            bg = bs_ref[0, :, g * N_STATE:(g + 1) * N_STATE].astype(BF16)
            parts = []
            for pr in range(hpg // 2):
                hi = g * hpg + 2 * pr
                xp = xs_ref[0, :, (hi // 2) * LANE:(hi // 2 + 1) * LANE]
                parts.append((xp * _pair_cols(w, lane, off + hi)).astype(BF16))
            xw = jnp.concatenate(parts, axis=1)
            st = lax.dot_general(xw, bg, tn_dims, preferred_element_type=F32)
            for hh in range(hpg):
                hi = g * hpg + hh
                r0 = hi * P_B
                dec = jnp.exp(tot[:, off + hi:off + hi + 1])
                h_scr[r0:r0 + P_B, :] = h_scr[r0:r0 + P_B, :] * dec + st[hh * P_B:(hh + 1) * P_B, :]

    @pl.when(ph == 0)
    def _():
        c = nc - 1 - k

        @pl.when(k == 0)
        def _():
            init_state(1)

        suf = _cumsum_rows(la, True)
        hb_scr[c] = h_scr[...]
        tot = suf[0:1, :]
        update_state(jnp.exp(tot - suf) * dtp, tot, H_B)

        @pl.when(k == nc - 1)
        def _():
            hfin_ref[0, 1] = h_scr[...]

    @pl.when(ph == 1)
    def _():
        @pl.when(k == 0)
        def _():
            init_state(0)

        pre = _cumsum_rows(la, False)
        suf = _cumsum_rows(la, True)
        pre_t = pre.T
        suf_t = suf.T
        dt_t = dtp.T
        e_pre = jnp.exp(pre)
        e_suf = jnp.exp(suf)
        lower = row >= lane
        upper = row <= lane
        hb = hb_scr[k]
        for g in range(G_B):
            cg = cs_ref[0, :, g * N_STATE:(g + 1) * N_STATE].astype(BF16)
            bg = bs_ref[0, :, g * N_STATE:(g + 1) * N_STATE].astype(BF16)
            cb = lax.dot_general(cg, bg, nt_dims, preferred_element_type=F32)
            hf_g = h_scr[g * gw:(g + 1) * gw, :].astype(BF16)
            hb_g = hb[g * gw:(g + 1) * gw, :].astype(BF16)
            y_f = lax.dot_general(cg, hf_g, nt_dims, preferred_element_type=F32)
            y_b = lax.dot_general(cg, hb_g, nt_dims, preferred_element_type=F32)
            for pr in range(hpg // 2):
                hi0 = g * hpg + 2 * pr
                cols = slice((hi0 // 2) * LANE, (hi0 // 2 + 1) * LANE)
                xp = xs_ref[0, :, cols]
                xpb = xp.astype(BF16)
                acc = None
                for e in range(2):
                    hi = hi0 + e
                    hb_i = H_B + hi
                    lf = jnp.exp(jnp.where(lower, pre[:, hi:hi + 1] - pre_t[hi:hi + 1, :], -jnp.inf))
                    lb = jnp.exp(jnp.where(upper, suf[:, hb_i:hb_i + 1] - suf_t[hb_i:hb_i + 1, :], -jnp.inf))
                    mm = cb * (lf * dt_t[hi:hi + 1, :] + lb * dt_t[hb_i:hb_i + 1, :])
                    keep = (lane < P_B) if e == 0 else (lane >= P_B)
                    xm = jnp.where(keep, xpb, jnp.zeros_like(xpb))
                    d = jnp.dot(mm.astype(BF16), xm, preferred_element_type=F32)
                    acc = d if acc is None else acc + d
                lo = pr * LANE
                y = (acc
                     + y_f[:, lo:lo + LANE] * _pair_cols(e_pre, lane, hi0)
                     + y_b[:, lo:lo + LANE] * _pair_cols(e_suf, lane, H_B + hi0)
                     + dsk_ref[:, cols] * xp)
                y_ref[0, :, cols] = y
        tot = pre[q - 1:q, :]
        update_state(jnp.exp(tot - pre) * dtp, tot, 0)

        @pl.when(k == nc - 1)
        def _():
            hfin_ref[0, 0] = h_scr[...]


def _ssd_call(xbc, proj3, dtb, alog, dsk, h0):
    b, seq = xbc.shape[0], xbc.shape[1]
    nc = seq // CHUNK
    has_h0 = h0 is not None

    def cidx(ph, k):
        return jnp.where(ph == 0, nc - 1 - k, k)

    in_specs = [
        pl.BlockSpec((1, CHUNK, D_INNER), lambda bi, ph, k: (bi, cidx(ph, k), 0)),
        pl.BlockSpec((1, CHUNK, G_B * N_STATE), lambda bi, ph, k: (bi, cidx(ph, k), D_INNER // (G_B * N_STATE))),
        pl.BlockSpec((1, CHUNK, G_B * N_STATE), lambda bi, ph, k: (bi, cidx(ph, k), D_INNER // (G_B * N_STATE) + 1)),
        pl.BlockSpec((1, CHUNK, LANE), lambda bi, ph, k: (bi, cidx(ph, k), C_DT // LANE)),
        pl.BlockSpec((1, LANE), lambda bi, ph, k: (0, 0)),
        pl.BlockSpec((1, LANE), lambda bi, ph, k: (0, 0)),
        pl.BlockSpec((1, D_INNER), lambda bi, ph, k: (0, 0)),
    ]
    args = [xbc, xbc, xbc, proj3, dtb, alog, dsk]
    if has_h0:
        in_specs.append(pl.BlockSpec((1, 2, D_INNER, N_STATE), lambda bi, ph, k: (bi, 0, 0, 0)))
        args.append(h0)
    return pl.pallas_call(
        functools.partial(_ssd_kernel, nc=nc, has_h0=has_h0),
        grid=(b, 2, nc),
        in_specs=in_specs,
        out_specs=[
            pl.BlockSpec((1, CHUNK, D_INNER), lambda bi, ph, k: (bi, ph * k, 0)),
            pl.BlockSpec((1, 2, D_INNER, N_STATE), lambda bi, ph, k: (bi, 0, 0, 0)),
        ],
        out_shape=[
            jax.ShapeDtypeStruct((b, seq, D_INNER), F32),
            jax.ShapeDtypeStruct((b, 2, D_INNER, N_STATE), F32),
        ],
        scratch_shapes=[
            pltpu.VMEM((D_INNER, N_STATE), F32),
            pltpu.VMEM((nc, D_INNER, N_STATE), F32),
        ],
        compiler_params=_cparams("parallel", "arbitrary", "arbitrary"),
        name="ssd",
    )(*args)


def _post_kernel(oa_ref, y_ref, z_ref, oc_ref, gate_ref, x_ref, mod_ref, gb_ref, g2_ref,
                 wa_ref, wb_ref, wc_ref, wo_ref, xo_ref, h2_ref):
    yb = _rms(y_ref[...] * _silu(z_ref[...]), gb_ref[...]).astype(BF16)
    out_a = jnp.dot(oa_ref[...], wa_ref[...], preferred_element_type=F32)
    out_b = jnp.dot(yb, wb_ref[...], preferred_element_type=F32)
    out_c = jnp.dot(oc_ref[...], wc_ref[...], preferred_element_type=F32)
    merged = (jax.nn.sigmoid(gate_ref[:, 0:D_MODEL]) * out_a
              + jax.nn.sigmoid(gate_ref[:, D_MODEL:2 * D_MODEL]) * out_b
              + jax.nn.sigmoid(gate_ref[:, 2 * D_MODEL:3 * D_MODEL]) * out_c)
    out = jnp.dot(merged.astype(BF16), wo_ref[...], preferred_element_type=F32)
    xn = x_ref[...] + mod_ref[0, 2:3, :] * out
    xo_ref[...] = xn
    h2 = _rms(xn, g2_ref[...]) * (1.0 + mod_ref[0, 4:5, :]) + mod_ref[0, 3:4, :]
    h2_ref[...] = h2.astype(BF16)


def _post_call(oa, y, proj, oc, x2d, mod, gb, g2, wa, wb, wc, wo, row_fn, tm):
    t = x2d.shape[0]
    full = lambda i: (0, 0)
    return pl.pallas_call(
        _post_kernel,
        grid=(t // tm,),
        in_specs=[
            pl.BlockSpec((tm, H_A * LANE), lambda i: (i, 0)),
            pl.BlockSpec((tm, D_INNER), lambda i: (i, 0)),
            pl.BlockSpec((tm, D_INNER), lambda i: (i, C_Z // D_INNER)),
            pl.BlockSpec((tm, H_C * DV_C), lambda i: (i, 0)),
            pl.BlockSpec((tm, 3 * D_MODEL), lambda i: (i, 0)),
            pl.BlockSpec((tm, D_MODEL), lambda i: (i, 0)),
            pl.BlockSpec((1, N_MOD, D_MODEL), lambda i: (row_fn(i, tm), 0, 0)),
            pl.BlockSpec((1, D_INNER), full),
            pl.BlockSpec((1, D_MODEL), full),
            pl.BlockSpec((H_A * LANE, D_MODEL), full),
            pl.BlockSpec((D_INNER, D_MODEL), full),
            pl.BlockSpec((H_C * DV_C, D_MODEL), full),
            pl.BlockSpec((D_MODEL, D_MODEL), full),
        ],
        out_specs=[
            pl.BlockSpec((tm, D_MODEL), lambda i: (i, 0)),
            pl.BlockSpec((tm, D_MODEL), lambda i: (i, 0)),
        ],
        out_shape=[
            jax.ShapeDtypeStruct((t, D_MODEL), F32),
            jax.ShapeDtypeStruct((t, D_MODEL), BF16),
        ],
        compiler_params=_cparams("parallel"),
        name="post_mix",
    )(oa, y, proj, oc, proj, x2d, mod, gb, g2, wa, wb, wc, wo)


def _matmul_kernel(x_ref, w_ref, o_ref):
    o_ref[...] = jnp.dot(x_ref[...], w_ref[...], preferred_element_type=F32)


def _ffn_up_call(h2, w, tm):
    t = h2.shape[0]
    n = 2 * D_FF
    tn = 512
    return pl.pallas_call(
        _matmul_kernel,
        grid=(t // tm, n // tn),
        in_specs=[
            pl.BlockSpec((tm, D_MODEL), lambda i, j: (i, 0)),
            pl.BlockSpec((D_MODEL, tn), lambda i, j: (0, j)),
        ],
        out_specs=pl.BlockSpec((tm, tn), lambda i, j: (i, j)),
        out_shape=jax.ShapeDtypeStruct((t, n), F32),
        compiler_params=_cparams("parallel", "arbitrary"),
        name="ffn_up",
    )(h2, w)


def _ffn_down_kernel(u_ref, prev_ref, next_ref, cw_ref, cb_ref, w_ref, x_ref, mod_ref, o_ref, *, tm, seq):
    u = _conv3(u_ref[...], prev_ref, next_ref, cw_ref, cb_ref, tm, seq)
    a = (_silu(u[:, :D_FF]) * u[:, D_FF:]).astype(BF16)
    out = jnp.dot(a, w_ref[...], preferred_element_type=F32)
    o_ref[...] = x_ref[...] + mod_ref[0, 5:6, :] * out


def _ffn_down_call(u, cw, cb, w, x2d, mod, row_fn, seq, tm):
    t = u.shape[0]
    n = 2 * D_FF
    full = lambda i: (0, 0)
    return pl.pallas_call(
        functools.partial(_ffn_down_kernel, tm=tm, seq=seq),
        grid=(t // tm,),
        in_specs=_halo_specs(tm, n, 0, t) + [
            pl.BlockSpec((3, n), full),
            pl.BlockSpec((1, n), full),
            pl.BlockSpec((D_FF, D_MODEL), full),
            pl.BlockSpec((tm, D_MODEL), lambda i: (i, 0)),
            pl.BlockSpec((1, N_MOD, D_MODEL), lambda i: (row_fn(i, tm), 0, 0)),
        ],
        out_specs=pl.BlockSpec((tm, D_MODEL), lambda i: (i, 0)),
        out_shape=jax.ShapeDtypeStruct((t, D_MODEL), F32),
        compiler_params=_cparams("parallel"),
        name="ffn_down",
    )(u, u, u, cw, cb, w, x2d, mod)


def _final_norm_kernel(x_ref, g_ref, o_ref):
    o_ref[...] = _rms(x_ref[...], g_ref[...])


def _final_norm_call(x2d, g, tm):
    t = x2d.shape[0]
    return pl.pallas_call(
        _final_norm_kernel,
        grid=(t // tm,),
        in_specs=[pl.BlockSpec((tm, D_MODEL), lambda i: (i, 0)), pl.BlockSpec((1, D_MODEL), lambda i: (0, 0))],
        out_specs=pl.BlockSpec((tm, D_MODEL), lambda i: (i, 0)),
        out_shape=jax.ShapeDtypeStruct((t, D_MODEL), F32),
        compiler_params=_cparams("parallel"),
        name="final_norm",
    )(x2d, g)


def _rope_tables(n_tokens, rot_dim, lead, tail):
    rows = n_tokens // GRID_W
    row = jnp.broadcast_to(jnp.arange(rows, dtype=F32)[:, None], (rows, GRID_W)).reshape(-1)
    col = jnp.broadcast_to(jnp.arange(GRID_W, dtype=F32)[None, :], (rows, GRID_W)).reshape(-1)
    n_freq = rot_dim // 4
    freqs = ROPE_THETA ** (-jnp.arange(n_freq, dtype=F32) / n_freq)
    ang = jnp.concatenate([row[:, None] * freqs, col[:, None] * freqs], axis=-1)
    cos, sin = jnp.cos(ang), jnp.sin(ang)
    c = jnp.repeat(cos, 2, axis=-1)
    s = jnp.stack([-sin, sin], axis=-1).reshape(n_tokens, rot_dim)
    reps = (LANE - lead - tail) // rot_dim
    c = jnp.tile(c, (1, reps))
    s = jnp.tile(s, (1, reps))
    c = jnp.concatenate([jnp.ones((n_tokens, lead), F32), c, jnp.ones((n_tokens, tail), F32)], axis=-1)
    s = jnp.concatenate([jnp.zeros((n_tokens, lead), F32), s, jnp.zeros((n_tokens, tail), F32)], axis=-1)
    return c, s


def _pad_w_in(w):
    cq, ckv, krope, z, xbc, dt, qc, kc, vc, gate = jnp.split(
        w, [384, 640, 672, 1696, 3744, 3776, 4288, 4800, 5312], axis=-1)
    d = w.shape[0]
    qc4 = qc.reshape(d, H_C, 2, DH_C)
    zq = jnp.zeros((d, H_C, DH_C), w.dtype)
    qc_pad = jnp.stack([jnp.concatenate([qc4[:, :, 0], zq], -1), jnp.concatenate([zq, qc4[:, :, 1]], -1)],
                       axis=2).reshape(d, 2 * H_C * LANE)
    zeros = lambda n: jnp.zeros((d, n), w.dtype)
    out = jnp.concatenate([gate, z, xbc, qc_pad, kc, vc, cq, zeros(NOPE_A), krope, zeros(LANE - NOPE_A - ROPE_A),
                           ckv, dt, zeros(LANE - 2 * H_B), zeros(LANE)], axis=-1)
    return out.astype(BF16)


def _prep_layer(l, w):
    d_q = NOPE_A + ROPE_A
    w_uq = w['w_uq'][l].reshape(Q_RANK, H_A, d_q)
    w_uq = jnp.pad(w_uq, ((0, 0), (0, 0), (0, LANE - d_q))).reshape(Q_RANK, H_A * LANE)
    w_ukv = w['w_ukv'][l].reshape(KV_RANK, H_A, NOPE_A + V_A)
    wk = jnp.pad(w_ukv[:, :, :NOPE_A], ((0, 0), (0, 0), (0, LANE - NOPE_A))).reshape(KV_RANK, H_A * LANE)
    wv = jnp.pad(w_ukv[:, :, NOPE_A:], ((0, 0), (0, 0), (0, LANE - V_A))).reshape(KV_RANK, H_A * LANE)
    w_oa = jnp.pad(w['w_oA'][l].reshape(H_A, V_A, D_MODEL), ((0, 0), (0, LANE - V_A), (0, 0)))
    pad_lane = lambda v: jnp.pad(v.reshape(1, -1), ((0, 0), (0, LANE - v.size)))
    return dict(
        w_in=_pad_w_in(w['w_in'][l]),
        norm1_g=w['norm1_g'][l].reshape(1, -1),
        norm2_g=w['norm2_g'][l].reshape(1, -1),
        q_norm_g=w['q_norm_g'][l].reshape(1, -1),
        kv_norm_g=w['kv_norm_g'][l].reshape(1, -1),
        w_uq=w_uq.astype(BF16),
        w_ukv=jnp.concatenate([wk, wv], axis=-1).astype(BF16),
        w_oA=w_oa.reshape(H_A * LANE, D_MODEL).astype(BF16),
        w_oB=w['w_oB'][l].astype(BF16),
        w_oC=w['w_oC'][l].astype(BF16),
        w_out=w['w_out'][l].astype(BF16),
        ssd_conv_w=w['ssd_conv_w'][l],
        ssd_conv_b=w['ssd_conv_b'][l].reshape(1, -1),
        dt_bias=pad_lane(w['ssd_dt_bias'][l]),
        a_log=pad_lane(w['ssd_a_log'][l]),
        d_skip=jnp.repeat(w['ssd_d_skip'][l], P_B).reshape(1, -1),
        ssd_norm_g=w['ssd_norm_g'][l].reshape(1, -1),
        diff_lambda=w['diff_lambda'][l],
        diff_norm_g=w['diff_norm_g'][l].reshape(1, -1),
        ffn_w_up=w['ffn_w_up'][l].astype(BF16),
        ffn_conv_w=w['ffn_conv_w'][l],
        ffn_conv_b=w['ffn_conv_b'][l].reshape(1, -1),
        ffn_w_down=w['ffn_w_down'][l].astype(BF16),
    )


def _tile(n, pref):
    t = min(n, pref)
    while n % t:
        t //= 2
    return t


def _block(x, mod, lp, lam_init, row_fn, tabs, ctx):
    b, seq, _ = x.shape
    t = b * seq
    x2d = x.reshape(t, D_MODEL)
    tm_big = _tile(seq, 1024)
    tm = _tile(seq, 256)
    proj = _inproj_call(x2d, lp['norm1_g'], mod, lp['w_in'], row_fn, tm_big)
    tab_a, tab_c = tabs if tabs is not None else (None, None)
    qa = _mla_q_call(proj, lp['q_norm_g'], lp['w_uq'], tab_a, seq, tm_big)
    kv_new = _kv_up_call(proj, C_CKV // KV_RANK, proj, C_KROPE // LANE, lp['kv_norm_g'], lp['w_ukv'],
                         tab_a, seq, tm_big)
    k_new, v_new, ckv_n = kv_new
    wide = H_A * LANE
    pieces_a = [(k_new.reshape(b, seq, wide), v_new.reshape(b, seq, wide))]
    qd, kd, vd = _dprep_call(proj, tab_c, seq, tm_big)
    pieces_c = [(kd.reshape(b, seq, H_C * LANE), vd.reshape(b, seq, H_C * DV_C))]
    h0 = None
    if ctx is not None:
        ctx_ckv, ctx_krope, ctx_k, ctx_v, h0 = ctx
        past = ctx_ckv.shape[1]
        kr_pad = jnp.pad(ctx_krope.reshape(b * past, ROPE_A), ((0, 0), (NOPE_A, LANE - NOPE_A - ROPE_A)))
        k_c, v_c = _kv_up_call(ctx_ckv.reshape(b * past, KV_RANK), 0, kr_pad, 0, None, lp['w_ukv'],
                               None, past, _tile(past, 1024))
        pieces_a = [(k_c.reshape(b, past, wide), v_c.reshape(b, past, wide))] + pieces_a
        pieces_c = [(ctx_k.reshape(b, past, H_C * LANE), ctx_v.reshape(b, past, H_C * DV_C))] + pieces_c
        h0 = h0.reshape(b, 2, D_INNER, N_STATE)
    oa = _attn_call(qa.reshape(b, seq, wide), pieces_a, H_A, False, _tile(seq, 512))
    oc = _attn_call(qd.reshape(b, seq, 2 * H_C * LANE), pieces_c, H_C, True, _tile(seq, 256),
                    extra=(lp['diff_lambda'], lp['diff_norm_g']), lam_init=lam_init)
    xbc = _ssd_conv_call(proj, lp['ssd_conv_w'], lp['ssd_conv_b'], seq, tm)
    y, hfin = _ssd_call(xbc.reshape(b, seq, CONV_CH), proj.reshape(b, seq, N_PROJ),
                        lp['dt_bias'], lp['a_log'], lp['d_skip'], h0)
    x2d, h2 = _post_call(oa.reshape(t, wide), y.reshape(t, D_INNER), proj, oc.reshape(t, H_C * DV_C), x2d, mod,
                         lp['ssd_norm_g'], lp['norm2_g'], lp['w_oA'], lp['w_oB'], lp['w_oC'], lp['w_out'],
                         row_fn, tm)
    u = _ffn_up_call(h2, lp['ffn_w_up'], tm_big)
    x2d = _ffn_down_call(u, lp['ffn_conv_w'], lp['ffn_conv_b'], lp['ffn_w_down'], x2d, mod, row_fn, seq, tm)
    st = None
    if ctx is None:
        st = (ckv_n.reshape(b, seq, KV_RANK),
              proj[:, C_KROPE + NOPE_A:C_KROPE + NOPE_A + ROPE_A].reshape(b, seq, ROPE_A),
              proj[:, C_KC:C_KC + H_C * 2 * DH_C].reshape(b, seq, H_C, 2, DH_C),
              proj[:, C_VC:C_VC + H_C * DV_C].reshape(b, seq, H_C, DV_C),
              hfin.reshape(b, 2, H_B, P_B, N_STATE))
    return x2d.reshape(b, seq, D_MODEL), st


def kernel(x_prompt, x_sample, cache_mla_ckv, cache_mla_krope, cache_diff_k, cache_diff_v, state_ssd, c, c_ctx, w_mod, b_mod, norm1_g, norm2_g, w_in, q_norm_g, kv_norm_g, w_uq, w_ukv, w_oA, ssd_conv_w, ssd_conv_b, ssd_dt_bias, ssd_a_log, ssd_d_skip, ssd_norm_g, w_oB, diff_lambda, diff_norm_g, w_oC, w_out, ffn_w_up, ffn_conv_w, ffn_conv_b, ffn_w_down, final_norm_g):
    weights = dict(norm1_g=norm1_g, norm2_g=norm2_g, w_in=w_in, q_norm_g=q_norm_g, kv_norm_g=kv_norm_g,
                   w_uq=w_uq, w_ukv=w_ukv, w_oA=w_oA, ssd_conv_w=ssd_conv_w, ssd_conv_b=ssd_conv_b,
                   ssd_dt_bias=ssd_dt_bias, ssd_a_log=ssd_a_log, ssd_d_skip=ssd_d_skip, ssd_norm_g=ssd_norm_g,
                   w_oB=w_oB, diff_lambda=diff_lambda, diff_norm_g=diff_norm_g, w_oC=w_oC, w_out=w_out,
                   ffn_w_up=ffn_w_up, ffn_conv_w=ffn_conv_w, ffn_conv_b=ffn_conv_b, ffn_w_down=ffn_w_down)
    n_dec, n_lat = x_sample.shape[0], x_sample.shape[1]
    ctx_row = n_dec
    cond = jnp.concatenate([c, c_ctx[None, :], jnp.zeros((COND_ROWS - n_dec - 1, D_MODEL), F32)], axis=0)
    mod_all = _mod_call(cond, w_mod, b_mod).reshape(DEPTH, COND_ROWS, N_MOD, D_MODEL)
    tabs = (_rope_tables(n_lat, ROPE_A, NOPE_A, LANE - NOPE_A - ROPE_A), _rope_tables(n_lat, DH_C, 0, 0))
    lat_row = lambda i, tm: (i * tm) // n_lat
    ctx_row_fn = lambda i, tm: ctx_row
    xp, xs = x_prompt, x_sample
    new = [[] for _ in range(5)]
    for l in range(DEPTH):
        lp = _prep_layer(l, weights)
        lam_init = 0.8 - 0.6 * math.exp(-0.3 * l)
        xp, st = _block(xp, mod_all[l], lp, lam_init, ctx_row_fn, None, None)
        for acc, v in zip(new, st):
            acc.append(v)
        ctx = (cache_mla_ckv[:, l], cache_mla_krope[:, l], cache_diff_k[:, l], cache_diff_v[:, l], state_ssd[:, l])
        xs, _ = _block(xs, mod_all[l], lp, lam_init, lat_row, tabs, ctx)
    bp, sp = xp.shape[0], xp.shape[1]
    y_prompt = _final_norm_call(xp.reshape(bp * sp, D_MODEL), final_norm_g.reshape(1, -1), _tile(bp * sp, 1024))
    y_sample = _final_norm_call(xs.reshape(n_dec * n_lat, D_MODEL), final_norm_g.reshape(1, -1),
                                _tile(n_dec * n_lat, 1024))
    return (y_prompt.reshape(bp, sp, D_MODEL), y_sample.reshape(n_dec, n_lat, D_MODEL),
            jnp.stack(new[0], axis=1), jnp.stack(new[1], axis=1), jnp.stack(new[2], axis=1),
            jnp.stack(new[3], axis=1), jnp.stack(new[4], axis=1))
```

```python
import functools
import math

import jax
import jax.numpy as jnp
from jax import lax
from jax.experimental import pallas as pl
from jax.experimental.pallas import tpu as pltpu

F32 = jnp.float32
BF16 = jnp.bfloat16

D_MODEL = 1024
DEPTH = 4
GRID_W = 64
ROPE_THETA = 10000.0
EPS = 1e-6
H_A, NOPE_A, ROPE_A, V_A = 8, 64, 32, 64
Q_RANK, KV_RANK = 384, 256
H_B, P_B, N_STATE, G_B, CHUNK = 16, 64, 128, 4, 128
D_INNER = H_B * P_B
CONV_CH = D_INNER + 2 * G_B * N_STATE
H_C, DH_C = 4, 64
DV_C = 2 * DH_C
D_FF = 2816
N_MOD = 6
COND_ROWS = 16

LOG2E = math.log2(math.e)
LANE = 128
SUBLANE = 8
VMEM_LIMIT = 56 * 1024 * 1024

C_GATE = 0
C_Z = 3072
C_XBC = 4096
C_QC = 6144
C_KC = 7168
C_VC = 7680
C_CQ = 8192
C_KROPE = 8576
C_CKV = 8704
C_DT = 8960
N_PROJ = 9216


def _cparams(*sem):
    return pltpu.CompilerParams(dimension_semantics=sem, vmem_limit_bytes=VMEM_LIMIT)


def _silu(x):
    return x * jax.nn.sigmoid(x)


def _rms(x, g):
    ms = jnp.mean(x * x, axis=-1, keepdims=True)
    return x * lax.rsqrt(ms + EPS) * g


def _rope(x, c, s):
    n = x.shape[-1]
    lane = lax.broadcasted_iota(jnp.int32, x.shape, x.ndim - 1)
    nxt = pltpu.roll(x, n - 1, x.ndim - 1)
    prv = pltpu.roll(x, 1, x.ndim - 1)
    return x * c + jnp.where(lane % 2 == 0, nxt, prv) * s


def _mod_kernel(c_ref, w_ref, b_ref, o_ref):
    s = _silu(c_ref[...]).astype(BF16)
    o_ref[0] = jnp.dot(s, w_ref[0].astype(BF16), preferred_element_type=F32) + b_ref[0]


def _mod_call(cond, w_mod, b_mod):
    tn = 1536
    n = N_MOD * D_MODEL
    return pl.pallas_call(
        _mod_kernel,
        grid=(DEPTH, n // tn),
        in_specs=[
            pl.BlockSpec((COND_ROWS, D_MODEL), lambda l, j: (0, 0)),
            pl.BlockSpec((1, D_MODEL, tn), lambda l, j: (l, 0, j)),
            pl.BlockSpec((1, 1, tn), lambda l, j: (l, 0, j)),
        ],
        out_specs=pl.BlockSpec((1, COND_ROWS, tn), lambda l, j: (l, 0, j)),
        out_shape=jax.ShapeDtypeStruct((DEPTH, COND_ROWS, n), F32),
        compiler_params=_cparams("parallel", "parallel"),
        name="mod",
    )(cond, w_mod, b_mod.reshape(DEPTH, 1, n))


def _inproj_kernel(x_ref, g_ref, mod_ref, w_ref, ws_ref, o_ref, os_ref, h_scr):
    @pl.when(pl.program_id(1) == 0)
    def _():
        h = _rms(x_ref[...], g_ref[...]) * (1.0 + mod_ref[0, 1:2, :]) + mod_ref[0, 0:1, :]
        hb = h.astype(BF16)
        h_scr[...] = hb
        os_ref[...] = jnp.dot(hb, ws_ref[...], preferred_element_type=F32)

    o_ref[...] = jnp.dot(h_scr[...], w_ref[...], preferred_element_type=F32).astype(BF16)


def _inproj_call(x2d, g, mod, w, w_side, row_fn, tm):
    t = x2d.shape[0]
    tn = 512
    n_side = w_side.shape[1]
    return pl.pallas_call(
        _inproj_kernel,
        grid=(t // tm, N_PROJ // tn),
        in_specs=[
            pl.BlockSpec((tm, D_MODEL), lambda i, j: (i, 0)),
            pl.BlockSpec((1, D_MODEL), lambda i, j: (0, 0)),
            pl.BlockSpec((1, N_MOD, D_MODEL), lambda i, j: (row_fn(i, tm), 0, 0)),
            pl.BlockSpec((D_MODEL, tn), lambda i, j: (0, j)),
            pl.BlockSpec((D_MODEL, n_side), lambda i, j: (0, 0)),
        ],
        out_specs=[
            pl.BlockSpec((tm, tn), lambda i, j: (i, j)),
            pl.BlockSpec((tm, n_side), lambda i, j: (i, 0)),
        ],
        out_shape=[
            jax.ShapeDtypeStruct((t, N_PROJ), BF16),
            jax.ShapeDtypeStruct((t, n_side), F32),
        ],
        scratch_shapes=[pltpu.VMEM((tm, D_MODEL), BF16)],
        compiler_params=_cparams("parallel", "arbitrary"),
        name="inproj",
    )(x2d, g, mod, w, w_side)


def _mla_q_kernel(*refs, rope, scale):
    if rope:
        p_ref, g_ref, w_ref, c_ref, s_ref, o_ref = refs
    else:
        p_ref, g_ref, w_ref, o_ref = refs
    cqn = _rms(p_ref[:, :Q_RANK].astype(F32), g_ref[...]).astype(BF16)
    q = jnp.dot(cqn, w_ref[...], preferred_element_type=F32)
    for h in range(H_A):
        qh = q[:, h * LANE:(h + 1) * LANE]
        if rope:
            qh = _rope(qh, c_ref[...], s_ref[...])
        o_ref[:, h * LANE:(h + 1) * LANE] = (qh * scale).astype(BF16)


def _mla_q_call(proj, g, w, tabs, seq, tm):
    t = proj.shape[0]
    rope = tabs is not None
    nb = seq // tm
    in_specs = [
        pl.BlockSpec((tm, 512), lambda i: (i, C_CQ // 512)),
        pl.BlockSpec((1, Q_RANK), lambda i: (0, 0)),
        pl.BlockSpec((Q_RANK, H_A * LANE), lambda i: (0, 0)),
    ]
    args = [proj, g, w]
    if rope:
        in_specs += [pl.BlockSpec((tm, LANE), lambda i: (i % nb, 0))] * 2
        args += list(tabs)
    return pl.pallas_call(
        functools.partial(_mla_q_kernel, rope=rope, scale=LOG2E * (NOPE_A + ROPE_A) ** -0.5),
        grid=(t // tm,),
        in_specs=in_specs,
        out_specs=pl.BlockSpec((tm, H_A * LANE), lambda i: (i, 0)),
        out_shape=jax.ShapeDtypeStruct((t, H_A * LANE), BF16),
        compiler_params=_cparams("parallel"),
        name="mla_q",
    )(*args)


def _kv_up_kernel(*refs, norm, rope, want_ckv):
    refs = list(refs)
    ckv_ref, kr_ref = refs[0], refs[1]
    pos = 2
    if norm:
        g_ref = refs[pos]
        pos += 1
    w_ref = refs[pos]
    pos += 1
    if rope:
        c_ref, s_ref = refs[pos], refs[pos + 1]
        pos += 2
    k_ref, v_ref = refs[pos], refs[pos + 1]
    ckv = ckv_ref[...].astype(F32)
    if norm:
        ckv = _rms(ckv, g_ref[...])
        if want_ckv:
            refs[pos + 2][...] = ckv
    kv = jnp.dot(ckv.astype(BF16), w_ref[...], preferred_element_type=F32)
    kr = kr_ref[...].astype(F32)
    if rope:
        kr = _rope(kr, c_ref[...], s_ref[...])
    for h in range(H_A):
        k_ref[:, h * LANE:(h + 1) * LANE] = (kv[:, h * LANE:(h + 1) * LANE] + kr).astype(BF16)
    v_ref[...] = kv[:, H_A * LANE:].astype(BF16)


def _kv_up_call(ckv_src, ckv_col, kr_src, kr_col, g, w, tabs, seq, tm, want_ckv=False):
    t = ckv_src.shape[0]
    norm = g is not None
    rope = tabs is not None
    nb = seq // tm
    in_specs = [
        pl.BlockSpec((tm, KV_RANK), lambda i: (i, ckv_col)),
        pl.BlockSpec((tm, LANE), lambda i: (i, kr_col)),
    ]
    args = [ckv_src, kr_src]
    if norm:
        in_specs.append(pl.BlockSpec((1, KV_RANK), lambda i: (0, 0)))
        args.append(g)
    in_specs.append(pl.BlockSpec((KV_RANK, 2 * H_A * LANE), lambda i: (0, 0)))
    args.append(w)
    if rope:
        in_specs += [pl.BlockSpec((tm, LANE), lambda i: (i % nb, 0))] * 2
        args += list(tabs)
    out_specs = [pl.BlockSpec((tm, H_A * LANE), lambda i: (i, 0))] * 2
    out_shape = [jax.ShapeDtypeStruct((t, H_A * LANE), BF16)] * 2
    if want_ckv:
        out_specs.append(pl.BlockSpec((tm, KV_RANK), lambda i: (i, 0)))
        out_shape.append(jax.ShapeDtypeStruct((t, KV_RANK), F32))
    return pl.pallas_call(
        functools.partial(_kv_up_kernel, norm=norm, rope=rope, want_ckv=want_ckv),
        grid=(t // tm,),
        in_specs=in_specs,
        out_specs=out_specs,
        out_shape=out_shape,
        compiler_params=_cparams("parallel"),
        name="kv_up",
    )(*args)


def _dprep_kernel(*refs, rope):
    if rope:
        q_ref, k_ref, c_ref, s_ref, qo_ref, ko_ref = refs
    else:
        q_ref, k_ref, qo_ref, ko_ref = refs
    for b in range(2 * H_C):
        q = q_ref[:, b * LANE:(b + 1) * LANE].astype(F32)
        if rope:
            q = _rope(q, c_ref[...], s_ref[...])
        qo_ref[:, b * LANE:(b + 1) * LANE] = (q * (LOG2E * DH_C ** -0.5)).astype(BF16)
    for b in range(H_C):
        k = k_ref[:, b * LANE:(b + 1) * LANE].astype(F32)
        if rope:
            k = _rope(k, c_ref[...], s_ref[...])
        ko_ref[:, b * LANE:(b + 1) * LANE] = k.astype(BF16)


def _dprep_call(proj, tabs, seq, tm):
    t = proj.shape[0]
    rope = tabs is not None
    nb = seq // tm
    in_specs = [
        pl.BlockSpec((tm, 1024), lambda i: (i, C_QC // 1024)),
        pl.BlockSpec((tm, 512), lambda i: (i, C_KC // 512)),
    ]
    args = [proj, proj]
    if rope:
        in_specs += [pl.BlockSpec((tm, LANE), lambda i: (i % nb, 0))] * 2
        args += list(tabs)
    return pl.pallas_call(
        functools.partial(_dprep_kernel, rope=rope),
        grid=(t // tm,),
        in_specs=in_specs,
        out_specs=[
            pl.BlockSpec((tm, 1024), lambda i: (i, 0)),
            pl.BlockSpec((tm, 512), lambda i: (i, 0)),
        ],
        out_shape=[
            jax.ShapeDtypeStruct((t, 1024), BF16),
            jax.ShapeDtypeStruct((t, 512), BF16),
        ],
        compiler_params=_cparams("parallel"),
        name="dprep",
    )(*args)


def _attn_kernel(*refs, chunks, n_pieces, diff, tq, lam_init):
    q_ref = refs[0]
    kv = refs[1:1 + 2 * n_pieces]
    o_ref = refs[-1]
    q = q_ref[0]
    if diff:
        q = jnp.concatenate([q[:, :LANE], q[:, LANE:]], axis=0)
    nt = (((1,), (1,)), ((), ()))
    m = l = acc = None
    for pi, st, sz in chunks:
        k = kv[2 * pi][0, st:st + sz, :].astype(BF16)
        v = kv[2 * pi + 1][0, st:st + sz, :].astype(BF16)
        s = lax.dot_general(q, k, nt, preferred_element_type=F32)
        ms = s.max(axis=-1, keepdims=True)
        if m is None:
            m = ms
            p = jnp.exp2(s - m)
            l = p.sum(axis=-1, keepdims=True)
            acc = jnp.dot(p.astype(BF16), v, preferred_element_type=F32)
        else:
            m_new = jnp.maximum(m, ms)
            alpha = jnp.exp2(m - m_new)
            p = jnp.exp2(s - m_new)
            l = alpha * l + p.sum(axis=-1, keepdims=True)
            acc = alpha * acc + jnp.dot(p.astype(BF16), v, preferred_element_type=F32)
            m = m_new
    o = acc / l
    if diff:
        lam_ref, g_ref = refs[1 + 2 * n_pieces], refs[2 + 2 * n_pieces]
        lv = lam_ref[...]
        lam = (jnp.exp(jnp.sum(lv[0:1] * lv[1:2], axis=-1, keepdims=True))
               - jnp.exp(jnp.sum(lv[2:3] * lv[3:4], axis=-1, keepdims=True)) + lam_init)
        oc = o[:tq] - lam * o[tq:]
        o = _rms(oc, g_ref[...]) * (1.0 - lam_init)
    o_ref[0] = o.astype(BF16)


KEY_CHUNK = 512


def _attn_call(q, pieces, n_heads, diff, tq, extra=(), lam_init=0.0):
    b, seq = q.shape[0], q.shape[1]
    qw = 2 * LANE if diff else LANE
    in_specs = [pl.BlockSpec((1, tq, qw), lambda bi, h, qi: (bi, qi, h))]
    args = [q]
    chunks = []
    for pi, (k, kc, v, vc) in enumerate(pieces):
        s = k.shape[1]
        in_specs += [pl.BlockSpec((1, s, LANE), lambda bi, h, qi, kc=kc: (bi, 0, kc + h)),
                     pl.BlockSpec((1, s, LANE), lambda bi, h, qi, vc=vc: (bi, 0, vc + h))]
        args += [k, v]
        step = _tile(s, KEY_CHUNK)
        chunks += [(pi, st, step) for st in range(0, s, step)]
    for e in extra:
        in_specs.append(pl.BlockSpec(e.shape, lambda bi, h, qi: (0, 0)))
        args.append(e)
    return pl.pallas_call(
        functools.partial(_attn_kernel, chunks=tuple(chunks), n_pieces=len(pieces), diff=diff, tq=tq,
                          lam_init=lam_init),
        grid=(b, n_heads, seq // tq),
        in_specs=in_specs,
        out_specs=pl.BlockSpec((1, tq, LANE), lambda bi, h, qi: (bi, qi, h)),
        out_shape=jax.ShapeDtypeStruct((b, seq, n_heads * LANE), BF16),
        compiler_params=_cparams("parallel", "parallel", "arbitrary"),
        name="diff_attn" if diff else "mla_attn",
    )(*args)


def _conv3(x, prev_ref, next_ref, w_ref, b_ref, tm, seq):
    i = pl.program_id(0)
    first = (i * tm) % seq == 0
    last = ((i + 1) * tm) % seq == 0
    pv = prev_ref[...].astype(F32)[prev_ref.shape[0] - 1:, :] * jnp.where(first, 0.0, 1.0)
    nx = next_ref[...].astype(F32)[0:1, :] * jnp.where(last, 0.0, 1.0)
    row = lax.broadcasted_iota(jnp.int32, x.shape, 0)
    xp = jnp.where(row == 0, pv, pltpu.roll(x, 1, 0))
    xn = jnp.where(row == tm - 1, nx, pltpu.roll(x, tm - 1, 0))
    return w_ref[0:1, :] * xp + w_ref[1:2, :] * x + w_ref[2:3, :] * xn + b_ref[...]


HALO = 16


def _halo_specs(tm, width, col, t):
    r = tm // HALO
    nblk = t // HALO
    return [
        pl.BlockSpec((tm, width), lambda i: (i, col)),
        pl.BlockSpec((HALO, width), lambda i: (jnp.maximum(i * r - 1, 0), col)),
        pl.BlockSpec((HALO, width), lambda i: (jnp.minimum((i + 1) * r, nblk - 1), col)),
    ]


def _ssd_conv_kernel(x_ref, prev_ref, next_ref, w_ref, b_ref, o_ref, *, tm, seq):
    o_ref[...] = _silu(_conv3(x_ref[...].astype(F32), prev_ref, next_ref, w_ref, b_ref, tm, seq))


def _ssd_conv_call(proj, w, b, seq, tm):
    t = proj.shape[0]
    return pl.pallas_call(
        functools.partial(_ssd_conv_kernel, tm=tm, seq=seq),
        grid=(t // tm,),
        in_specs=_halo_specs(tm, CONV_CH, C_XBC // CONV_CH, t) + [
            pl.BlockSpec((3, CONV_CH), lambda i: (0, 0)),
            pl.BlockSpec((1, CONV_CH), lambda i: (0, 0)),
        ],
        out_specs=pl.BlockSpec((tm, CONV_CH), lambda i: (i, 0)),
        out_shape=jax.ShapeDtypeStruct((t, CONV_CH), F32),
        compiler_params=_cparams("parallel"),
        name="ssd_conv",
    )(proj, proj, proj, w, b)


def _cumsum_rows(x, reverse):
    n = x.shape[0]
    row = lax.broadcasted_iota(jnp.int32, x.shape, 0)
    s = 1
    while s < n:
        if reverse:
            x = x + jnp.where(row < n - s, pltpu.roll(x, n - s, 0), 0.0)
        else:
            x = x + jnp.where(row >= s, pltpu.roll(x, s, 0), 0.0)
        s *= 2
    return x


def _pair_cols(v, lane, h):
    return jnp.where(lane < P_B, v[:, h:h + 1], v[:, h + 1:h + 2])


def _ssd_kernel(*refs, nc, has_h0):
    if has_h0:
        xs_ref, bs_ref, cs_ref, dt_ref, dtb_ref, alog_ref, dsk_ref, h0_ref = refs[:8]
        y_ref, hfin_ref, h_scr, hb_scr = refs[8:]
    else:
        xs_ref, bs_ref, cs_ref, dt_ref, dtb_ref, alog_ref, dsk_ref = refs[:7]
        y_ref, hfin_ref, h_scr, hb_scr = refs[7:]
    ph = pl.program_id(1)
    k = pl.program_id(2)
    q = CHUNK
    hpg = H_B // G_B
    gw = hpg * P_B
    dtp = jax.nn.softplus(dt_ref[0] + dtb_ref[...])
    la = dtp * (-jnp.exp(alog_ref[...]))
    lane = lax.broadcasted_iota(jnp.int32, (q, LANE), 1)
    row = lax.broadcasted_iota(jnp.int32, (q, LANE), 0)
    tn_dims = (((0,), (0,)), ((), ()))
    nt_dims = (((1,), (1,)), ((), ()))

    def init_state(d):
        if has_h0:
            h_scr[...] = h0_ref[0, d]
        else:
            h_scr[...] = jnp.zeros_like(h_scr)

    def update_state(w, tot, off):
        for g in range(G_B):
            bg = bs_ref[0, :, g * N_STATE:(g + 1) * N_STATE].astype(BF16)
            parts = []
            for pr in range(hpg // 2):
                hi = g * hpg + 2 * pr
                xp = xs_ref[0, :, (hi // 2) * LANE:(hi // 2 + 1) * LANE]
                parts.append((xp * _pair_cols(w, lane, off + hi)).astype(BF16))
            xw = jnp.concatenate(parts, axis=1)
            st = lax.dot_general(xw, bg, tn_dims, preferred_element_type=F32)
            for hh in range(hpg):
                hi = g * hpg + hh
                r0 = hi * P_B
                dec = jnp.exp(tot[:, off + hi:off + hi + 1])
                h_scr[r0:r0 + P_B, :] = h_scr[r0:r0 + P_B, :] * dec + st[hh * P_B:(hh + 1) * P_B, :]

    @pl.when(ph == 0)
    def _():
        c = nc - 1 - k

        @pl.when(k == 0)
        def _():
            init_state(1)

        suf = _cumsum_rows(la, True)
        hb_scr[c] = h_scr[...]
        tot = suf[0:1, :]
        update_state(jnp.exp(tot - suf) * dtp, tot, H_B)

        @pl.when(k == nc - 1)
        def _():
            hfin_ref[0, 1] = h_scr[...]

    @pl.when(ph == 1)
    def _():
        @pl.when(k == 0)
        def _():
            init_state(0)

        pre = _cumsum_rows(la, False)
        suf = _cumsum_rows(la, True)
        pre_t = pre.T
        suf_t = suf.T
        dt_t = dtp.T
        e_pre = jnp.exp(pre)
        e_suf = jnp.exp(suf)
        lower = row >= lane
        upper = row <= lane
        hb = hb_scr[k]
        for g in range(G_B):
            cg = cs_ref[0, :, g * N_STATE:(g + 1) * N_STATE].astype(BF16)
            bg = bs_ref[0, :, g * N_STATE:(g + 1) * N_STATE].astype(BF16)
            cb = lax.dot_general(cg, bg, nt_dims, preferred_element_type=F32)
            hf_g = h_scr[g * gw:(g + 1) * gw, :].astype(BF16)
            hb_g = hb[g * gw:(g + 1) * gw, :].astype(BF16)
            y_f = lax.dot_general(cg, hf_g, nt_dims, preferred_element_type=F32)
            y_b = lax.dot_general(cg, hb_g, nt_dims, preferred_element_type=F32)
            for pr in range(hpg // 2):
                hi0 = g * hpg + 2 * pr
                cols = slice((hi0 // 2) * LANE, (hi0 // 2 + 1) * LANE)
                xp = xs_ref[0, :, cols]
                xpb = xp.astype(BF16)
                acc = None
                for e in range(2):
                    hi = hi0 + e
                    hb_i = H_B + hi
                    lf = jnp.exp(jnp.where(lower, pre[:, hi:hi + 1] - pre_t[hi:hi + 1, :], -jnp.inf))
                    lb = jnp.exp(jnp.where(upper, suf[:, hb_i:hb_i + 1] - suf_t[hb_i:hb_i + 1, :], -jnp.inf))
                    mm = cb * (lf * dt_t[hi:hi + 1, :] + lb * dt_t[hb_i:hb_i + 1, :])
                    keep = (lane < P_B) if e == 0 else (lane >= P_B)
                    xm = jnp.where(keep, xpb, jnp.zeros_like(xpb))
                    d = jnp.dot(mm.astype(BF16), xm, preferred_element_type=F32)
                    acc = d if acc is None else acc + d
                lo = pr * LANE
                y = (acc
                     + y_f[:, lo:lo + LANE] * _pair_cols(e_pre, lane, hi0)
                     + y_b[:, lo:lo + LANE] * _pair_cols(e_suf, lane, H_B + hi0)
                     + dsk_ref[:, cols] * xp)
                y_ref[0, :, cols] = y
        tot = pre[q - 1:q, :]
        update_state(jnp.exp(tot - pre) * dtp, tot, 0)

        @pl.when(k == nc - 1)
        def _():
            hfin_ref[0, 0] = h_scr[...]


def _ssd_call(xbc, dt_src, dt_col, dtb, alog, dsk, h0):
    b, seq = xbc.shape[0], xbc.shape[1]
    nc = seq // CHUNK
    has_h0 = h0 is not None

    def cidx(ph, k):
        return jnp.where(ph == 0, nc - 1 - k, k)

    in_specs = [
        pl.BlockSpec((1, CHUNK, D_INNER), lambda bi, ph, k: (bi, cidx(ph, k), 0)),
        pl.BlockSpec((1, CHUNK, G_B * N_STATE), lambda bi, ph, k: (bi, cidx(ph, k), D_INNER // (G_B * N_STATE))),
        pl.BlockSpec((1, CHUNK, G_B * N_STATE), lambda bi, ph, k: (bi, cidx(ph, k), D_INNER // (G_B * N_STATE) + 1)),
        pl.BlockSpec((1, CHUNK, LANE), lambda bi, ph, k: (bi, cidx(ph, k), dt_col)),
        pl.BlockSpec((1, LANE), lambda bi, ph, k: (0, 0)),
        pl.BlockSpec((1, LANE), lambda bi, ph, k: (0, 0)),
        pl.BlockSpec((1, D_INNER), lambda bi, ph, k: (0, 0)),
    ]
    args = [xbc, xbc, xbc, dt_src, dtb, alog, dsk]
    if has_h0:
        in_specs.append(pl.BlockSpec((1, 2, D_INNER, N_STATE), lambda bi, ph, k: (bi, 0, 0, 0)))
        args.append(h0)
    return pl.pallas_call(
        functools.partial(_ssd_kernel, nc=nc, has_h0=has_h0),
        grid=(b, 2, nc),
        in_specs=in_specs,
        out_specs=[
            pl.BlockSpec((1, CHUNK, D_INNER), lambda bi, ph, k: (bi, ph * k, 0)),
            pl.BlockSpec((1, 2, D_INNER, N_STATE), lambda bi, ph, k: (bi, 0, 0, 0)),
        ],
        out_shape=[
            jax.ShapeDtypeStruct((b, seq, D_INNER), F32),
            jax.ShapeDtypeStruct((b, 2, D_INNER, N_STATE), F32),
        ],
        scratch_shapes=[
            pltpu.VMEM((D_INNER, N_STATE), F32),
            pltpu.VMEM((nc, D_INNER, N_STATE), F32),
        ],
        compiler_params=_cparams("parallel", "arbitrary", "arbitrary"),
        name="ssd",
    )(*args)


def _post_kernel(oa_ref, y_ref, z_ref, oc_ref, gate_ref, x_ref, mod_ref, gb_ref, g2_ref,
                 wa_ref, wb_ref, wc_ref, wo_ref, xo_ref, h2_ref):
    yb = _rms(y_ref[...] * _silu(z_ref[...].astype(F32)), gb_ref[...]).astype(BF16)
    out_a = jnp.dot(oa_ref[...], wa_ref[...], preferred_element_type=F32)
    out_b = jnp.dot(yb, wb_ref[...], preferred_element_type=F32)
    out_c = jnp.dot(oc_ref[...], wc_ref[...], preferred_element_type=F32)
    merged = (jax.nn.sigmoid(gate_ref[:, 0:D_MODEL].astype(F32)) * out_a
              + jax.nn.sigmoid(gate_ref[:, D_MODEL:2 * D_MODEL].astype(F32)) * out_b
              + jax.nn.sigmoid(gate_ref[:, 2 * D_MODEL:3 * D_MODEL].astype(F32)) * out_c)
    out = jnp.dot(merged.astype(BF16), wo_ref[...], preferred_element_type=F32)
    xn = x_ref[...] + mod_ref[0, 2:3, :] * out
    xo_ref[...] = xn
    h2 = _rms(xn, g2_ref[...]) * (1.0 + mod_ref[0, 4:5, :]) + mod_ref[0, 3:4, :]
    h2_ref[...] = h2.astype(BF16)


def _post_call(oa, y, proj, oc, x2d, mod, gb, g2, wa, wb, wc, wo, row_fn, tm):
    t = x2d.shape[0]
    full = lambda i: (0, 0)
    return pl.pallas_call(
        _post_kernel,
        grid=(t // tm,),
        in_specs=[
            pl.BlockSpec((tm, H_A * LANE), lambda i: (i, 0)),
            pl.BlockSpec((tm, D_INNER), lambda i: (i, 0)),
            pl.BlockSpec((tm, D_INNER), lambda i: (i, C_Z // D_INNER)),
            pl.BlockSpec((tm, H_C * DV_C), lambda i: (i, 0)),
            pl.BlockSpec((tm, 3 * D_MODEL), lambda i: (i, 0)),
            pl.BlockSpec((tm, D_MODEL), lambda i: (i, 0)),
            pl.BlockSpec((1, N_MOD, D_MODEL), lambda i: (row_fn(i, tm), 0, 0)),
            pl.BlockSpec((1, D_INNER), full),
            pl.BlockSpec((1, D_MODEL), full),
            pl.BlockSpec((H_A * LANE, D_MODEL), full),
            pl.BlockSpec((D_INNER, D_MODEL), full),
            pl.BlockSpec((H_C * DV_C, D_MODEL), full),
            pl.BlockSpec((D_MODEL, D_MODEL), full),
        ],
        out_specs=[
            pl.BlockSpec((tm, D_MODEL), lambda i: (i, 0)),
            pl.BlockSpec((tm, D_MODEL), lambda i: (i, 0)),
        ],
        out_shape=[
            jax.ShapeDtypeStruct((t, D_MODEL), F32),
            jax.ShapeDtypeStruct((t, D_MODEL), BF16),
        ],
        compiler_params=_cparams("parallel"),
        name="post_mix",
    )(oa, y, proj, oc, proj, x2d, mod, gb, g2, wa, wb, wc, wo)


def _matmul_kernel(x_ref, w_ref, o_ref):
    o_ref[...] = jnp.dot(x_ref[...], w_ref[...], preferred_element_type=F32).astype(o_ref.dtype)


def _ffn_up_call(h2, w, tm):
    t = h2.shape[0]
    n = 2 * D_FF
    tn = 512
    return pl.pallas_call(
        _matmul_kernel,
        grid=(t // tm, n // tn),
        in_specs=[
            pl.BlockSpec((tm, D_MODEL), lambda i, j: (i, 0)),
            pl.BlockSpec((D_MODEL, tn), lambda i, j: (0, j)),
        ],
        out_specs=pl.BlockSpec((tm, tn), lambda i, j: (i, j)),
        out_shape=jax.ShapeDtypeStruct((t, n), BF16),
        compiler_params=_cparams("parallel", "arbitrary"),
        name="ffn_up",
    )(h2, w)


def _ffn_down_kernel(u_ref, prev_ref, next_ref, cw_ref, cb_ref, w_ref, x_ref, mod_ref, o_ref, *, tm, seq):
    u = _conv3(u_ref[...].astype(F32), prev_ref, next_ref, cw_ref, cb_ref, tm, seq)
    a = (_silu(u[:, :D_FF]) * u[:, D_FF:]).astype(BF16)
    out = jnp.dot(a, w_ref[...], preferred_element_type=F32)
    o_ref[...] = x_ref[...] + mod_ref[0, 5:6, :] * out


def _ffn_down_call(u, cw, cb, w, x2d, mod, row_fn, seq, tm):
    t = u.shape[0]
    n = 2 * D_FF
    full = lambda i: (0, 0)
    return pl.pallas_call(
        functools.partial(_ffn_down_kernel, tm=tm, seq=seq),
        grid=(t // tm,),
        in_specs=_halo_specs(tm, n, 0, t) + [
            pl.BlockSpec((3, n), full),
            pl.BlockSpec((1, n), full),
            pl.BlockSpec((D_FF, D_MODEL), full),
            pl.BlockSpec((tm, D_MODEL), lambda i: (i, 0)),
            pl.BlockSpec((1, N_MOD, D_MODEL), lambda i: (row_fn(i, tm), 0, 0)),
        ],
        out_specs=pl.BlockSpec((tm, D_MODEL), lambda i: (i, 0)),
        out_shape=jax.ShapeDtypeStruct((t, D_MODEL), F32),
        compiler_params=_cparams("parallel"),
        name="ffn_down",
    )(u, u, u, cw, cb, w, x2d, mod)


def _final_norm_kernel(x_ref, g_ref, o_ref):
    o_ref[...] = _rms(x_ref[...], g_ref[...])


def _final_norm_call(x2d, g, tm):
    t = x2d.shape[0]
    return pl.pallas_call(
        _final_norm_kernel,
        grid=(t // tm,),
        in_specs=[pl.BlockSpec((tm, D_MODEL), lambda i: (i, 0)), pl.BlockSpec((1, D_MODEL), lambda i: (0, 0))],
        out_specs=pl.BlockSpec((tm, D_MODEL), lambda i: (i, 0)),
        out_shape=jax.ShapeDtypeStruct((t, D_MODEL), F32),
        compiler_params=_cparams("parallel"),
        name="final_norm",
    )(x2d, g)


def _rope_tables(n_tokens, rot_dim, lead, tail):
    rows = n_tokens // GRID_W
    row = jnp.broadcast_to(jnp.arange(rows, dtype=F32)[:, None], (rows, GRID_W)).reshape(-1)
    col = jnp.broadcast_to(jnp.arange(GRID_W, dtype=F32)[None, :], (rows, GRID_W)).reshape(-1)
    n_freq = rot_dim // 4
    freqs = ROPE_THETA ** (-jnp.arange(n_freq, dtype=F32) / n_freq)
    ang = jnp.concatenate([row[:, None] * freqs, col[:, None] * freqs], axis=-1)
    cos, sin = jnp.cos(ang), jnp.sin(ang)
    c = jnp.repeat(cos, 2, axis=-1)
    s = jnp.stack([-sin, sin], axis=-1).reshape(n_tokens, rot_dim)
    reps = (LANE - lead - tail) // rot_dim
    c = jnp.tile(c, (1, reps))
    s = jnp.tile(s, (1, reps))
    c = jnp.concatenate([jnp.ones((n_tokens, lead), F32), c, jnp.ones((n_tokens, tail), F32)], axis=-1)
    s = jnp.concatenate([jnp.zeros((n_tokens, lead), F32), s, jnp.zeros((n_tokens, tail), F32)], axis=-1)
    return c, s


def _pad_w_in(w):
    cq, ckv, krope, z, xbc, dt, qc, kc, vc, gate = jnp.split(
        w, [384, 640, 672, 1696, 3744, 3776, 4288, 4800, 5312], axis=-1)
    d = w.shape[0]
    qc4 = qc.reshape(d, H_C, 2, DH_C)
    zq = jnp.zeros((d, H_C, DH_C), w.dtype)
    qc_pad = jnp.stack([jnp.concatenate([qc4[:, :, 0], zq], -1), jnp.concatenate([zq, qc4[:, :, 1]], -1)],
                       axis=2).reshape(d, 2 * H_C * LANE)
    zeros = lambda n: jnp.zeros((d, n), w.dtype)
    kr_pad = jnp.concatenate([zeros(NOPE_A), krope, zeros(LANE - NOPE_A - ROPE_A)], axis=-1)
    dt_pad = jnp.concatenate([dt, zeros(LANE - 2 * H_B)], axis=-1)
    main = jnp.concatenate([gate, z, xbc, qc_pad, kc, vc, cq, kr_pad, ckv, dt_pad, zeros(LANE)], axis=-1)
    side_ctx = jnp.concatenate([kc, vc, ckv, kr_pad, dt_pad], axis=-1)
    return main.astype(BF16), side_ctx.astype(BF16), dt_pad.astype(BF16)


S_KC, S_VC, S_CKV, S_KROPE, S_DT = 0, 512, 1024, 1280, 1408


def _prep_layer(l, w):
    d_q = NOPE_A + ROPE_A
    w_uq = w['w_uq'][l].reshape(Q_RANK, H_A, d_q)
    w_uq = jnp.pad(w_uq, ((0, 0), (0, 0), (0, LANE - d_q))).reshape(Q_RANK, H_A * LANE)
    w_ukv = w['w_ukv'][l].reshape(KV_RANK, H_A, NOPE_A + V_A)
    wk = jnp.pad(w_ukv[:, :, :NOPE_A], ((0, 0), (0, 0), (0, LANE - NOPE_A))).reshape(KV_RANK, H_A * LANE)
    wv = jnp.pad(w_ukv[:, :, NOPE_A:], ((0, 0), (0, 0), (0, LANE - V_A))).reshape(KV_RANK, H_A * LANE)
    w_oa = jnp.pad(w['w_oA'][l].reshape(H_A, V_A, D_MODEL), ((0, 0), (0, LANE - V_A), (0, 0)))
    pad_lane = lambda v: jnp.pad(v.reshape(1, -1), ((0, 0), (0, LANE - v.size)))
    w_in, w_side_ctx, w_side_lat = _pad_w_in(w['w_in'][l])
    return dict(
        w_in=w_in, w_side_ctx=w_side_ctx, w_side_lat=w_side_lat,
        norm1_g=w['norm1_g'][l].reshape(1, -1),
        norm2_g=w['norm2_g'][l].reshape(1, -1),
        q_norm_g=w['q_norm_g'][l].reshape(1, -1),
        kv_norm_g=w['kv_norm_g'][l].reshape(1, -1),
        w_uq=w_uq.astype(BF16),
        w_ukv=jnp.concatenate([wk, wv], axis=-1).astype(BF16),
        w_oA=w_oa.reshape(H_A * LANE, D_MODEL).astype(BF16),
        w_oB=w['w_oB'][l].astype(BF16),
        w_oC=w['w_oC'][l].astype(BF16),
        w_out=w['w_out'][l].astype(BF16),
        ssd_conv_w=w['ssd_conv_w'][l],
        ssd_conv_b=w['ssd_conv_b'][l].reshape(1, -1),
        dt_bias=pad_lane(w['ssd_dt_bias'][l]),
        a_log=pad_lane(w['ssd_a_log'][l]),
        d_skip=jnp.repeat(w['ssd_d_skip'][l], P_B).reshape(1, -1),
        ssd_norm_g=w['ssd_norm_g'][l].reshape(1, -1),
        diff_lambda=w['diff_lambda'][l],
        diff_norm_g=w['diff_norm_g'][l].reshape(1, -1),
        ffn_w_up=w['ffn_w_up'][l].astype(BF16),
        ffn_conv_w=w['ffn_conv_w'][l],
        ffn_conv_b=w['ffn_conv_b'][l].reshape(1, -1),
        ffn_w_down=w['ffn_w_down'][l].astype(BF16),
    )


def _tile(n, pref):
    t = min(n, pref)
    while n % t:
        t //= 2
    return t


def _block(x, mod, lp, lam_init, row_fn, tabs, ctx):
    b, seq, _ = x.shape
    t = b * seq
    x2d = x.reshape(t, D_MODEL)
    is_ctx = ctx is None
    tm_big = _tile(t if is_ctx else seq, 1024)
    tm = _tile(seq, 256)
    proj, side = _inproj_call(x2d, lp['norm1_g'], mod, lp['w_in'],
                              lp['w_side_ctx'] if is_ctx else lp['w_side_lat'], row_fn, tm_big)
    proj3 = proj.reshape(b, seq, N_PROJ)
    tab_a, tab_c = tabs if tabs is not None else (None, None)
    qa = _mla_q_call(proj, lp['q_norm_g'], lp['w_uq'], tab_a, seq, tm_big)
    if is_ctx:
        k_new, v_new, ckv_n = _kv_up_call(side, S_CKV // KV_RANK, side, S_KROPE // LANE, lp['kv_norm_g'],
                                          lp['w_ukv'], None, seq, tm_big, want_ckv=True)
    else:
        k_new, v_new = _kv_up_call(proj, C_CKV // KV_RANK, proj, C_KROPE // LANE, lp['kv_norm_g'], lp['w_ukv'],
                                   tab_a, seq, tm_big)
    wide = H_A * LANE
    pieces_a = [(k_new.reshape(b, seq, wide), 0, v_new.reshape(b, seq, wide), 0)]
    qd, kd = _dprep_call(proj, tab_c, seq, tm_big)
    pieces_c = [(kd.reshape(b, seq, H_C * LANE), 0, proj3, C_VC // LANE)]
    h0 = None
    if not is_ctx:
        ctx_ckv, ctx_krope, ctx_k, ctx_v, h0 = ctx
        past = ctx_ckv.shape[1]
        kr_pad = jnp.pad(ctx_krope.reshape(b * past, ROPE_A), ((0, 0), (NOPE_A, LANE - NOPE_A - ROPE_A)))
        k_c, v_c = _kv_up_call(ctx_ckv.reshape(b * past, KV_RANK), 0, kr_pad, 0, None, lp['w_ukv'],
                               None, past, _tile(b * past, 1024))
        pieces_a = [(k_c.reshape(b, past, wide), 0, v_c.reshape(b, past, wide), 0)] + pieces_a
        pieces_c = [(ctx_k.reshape(b, past, H_C * LANE), 0, ctx_v.reshape(b, past, H_C * DV_C), 0)] + pieces_c
        h0 = h0.reshape(b, 2, D_INNER, N_STATE)
    oa = _attn_call(qa.reshape(b, seq, wide), pieces_a, H_A, False, _tile(seq, 512))
    oc = _attn_call(qd.reshape(b, seq, 2 * H_C * LANE), pieces_c, H_C, True, _tile(seq, 256),
                    extra=(lp['diff_lambda'], lp['diff_norm_g']), lam_init=lam_init)
    xbc = _ssd_conv_call(proj, lp['ssd_conv_w'], lp['ssd_conv_b'], seq, tm)
    y, hfin = _ssd_call(xbc.reshape(b, seq, CONV_CH), side.reshape(b, seq, side.shape[1]),
                        S_DT // LANE if is_ctx else 0, lp['dt_bias'], lp['a_log'], lp['d_skip'], h0)
    x2d, h2 = _post_call(oa.reshape(t, wide), y.reshape(t, D_INNER), proj, oc.reshape(t, H_C * DV_C), x2d, mod,
                         lp['ssd_norm_g'], lp['norm2_g'], lp['w_oA'], lp['w_oB'], lp['w_oC'], lp['w_out'],
                         row_fn, tm)
    u = _ffn_up_call(h2, lp['ffn_w_up'], tm_big)
    x2d = _ffn_down_call(u, lp['ffn_conv_w'], lp['ffn_conv_b'], lp['ffn_w_down'], x2d, mod, row_fn, seq, tm)
    st = None
    if is_ctx:
        st = (ckv_n.reshape(b, seq, KV_RANK),
              side[:, S_KROPE + NOPE_A:S_KROPE + NOPE_A + ROPE_A].reshape(b, seq, ROPE_A),
              side[:, S_KC:S_KC + H_C * 2 * DH_C].reshape(b, seq, H_C, 2, DH_C),
              side[:, S_VC:S_VC + H_C * DV_C].reshape(b, seq, H_C, DV_C),
              hfin.reshape(b, 2, H_B, P_B, N_STATE))
    return x2d.reshape(b, seq, D_MODEL), st


def kernel(x_prompt, x_sample, cache_mla_ckv, cache_mla_krope, cache_diff_k, cache_diff_v, state_ssd, c, c_ctx, w_mod, b_mod, norm1_g, norm2_g, w_in, q_norm_g, kv_norm_g, w_uq, w_ukv, w_oA, ssd_conv_w, ssd_conv_b, ssd_dt_bias, ssd_a_log, ssd_d_skip, ssd_norm_g, w_oB, diff_lambda, diff_norm_g, w_oC, w_out, ffn_w_up, ffn_conv_w, ffn_conv_b, ffn_w_down, final_norm_g):
    weights = dict(norm1_g=norm1_g, norm2_g=norm2_g, w_in=w_in, q_norm_g=q_norm_g, kv_norm_g=kv_norm_g,
                   w_uq=w_uq, w_ukv=w_ukv, w_oA=w_oA, ssd_conv_w=ssd_conv_w, ssd_conv_b=ssd_conv_b,
                   ssd_dt_bias=ssd_dt_bias, ssd_a_log=ssd_a_log, ssd_d_skip=ssd_d_skip, ssd_norm_g=ssd_norm_g,
                   w_oB=w_oB, diff_lambda=diff_lambda, diff_norm_g=diff_norm_g, w_oC=w_oC, w_out=w_out,
                   ffn_w_up=ffn_w_up, ffn_conv_w=ffn_conv_w, ffn_conv_b=ffn_conv_b, ffn_w_down=ffn_w_down)
    n_dec, n_lat = x_sample.shape[0], x_sample.shape[1]
    ctx_row = n_dec
    cond = jnp.concatenate([c, c_ctx[None, :], jnp.zeros((COND_ROWS - n_dec - 1, D_MODEL), F32)], axis=0)
    mod_all = _mod_call(cond, w_mod, b_mod).reshape(DEPTH, COND_ROWS, N_MOD, D_MODEL)
    tabs = (_rope_tables(n_lat, ROPE_A, NOPE_A, LANE - NOPE_A - ROPE_A), _rope_tables(n_lat, DH_C, 0, 0))
    lat_row = lambda i, tm: (i * tm) // n_lat
    ctx_row_fn = lambda i, tm: ctx_row
    xp, xs = x_prompt, x_sample
    new = [[] for _ in range(5)]
    for l in range(DEPTH):
        lp = _prep_layer(l, weights)
        lam_init = 0.8 - 0.6 * math.exp(-0.3 * l)
        xp, st = _block(xp, mod_all[l], lp, lam_init, ctx_row_fn, None, None)
        for acc, v in zip(new, st):
            acc.append(v)
        ctx = (cache_mla_ckv[:, l], cache_mla_krope[:, l], cache_diff_k[:, l], cache_diff_v[:, l], state_ssd[:, l])
        xs, _ = _block(xs, mod_all[l], lp, lam_init, lat_row, tabs, ctx)
    bp, sp = xp.shape[0], xp.shape[1]
    y_prompt = _final_norm_call(xp.reshape(bp * sp, D_MODEL), final_norm_g.reshape(1, -1), _tile(bp * sp, 1024))
    y_sample = _final_norm_call(xs.reshape(n_dec * n_lat, D_MODEL), final_norm_g.reshape(1, -1),
                                _tile(n_dec * n_lat, 1024))
    return (y_prompt.reshape(bp, sp, D_MODEL), y_sample.reshape(n_dec, n_lat, D_MODEL),
            jnp.stack(new[0], axis=1), jnp.stack(new[1], axis=1), jnp.stack(new[2], axis=1),
            jnp.stack(new[3], axis=1), jnp.stack(new[4], axis=1))
```

```python
import functools
import math

import jax
import jax.numpy as jnp
from jax import lax
from jax.experimental import pallas as pl
from jax.experimental.pallas import tpu as pltpu

F32 = jnp.float32
BF16 = jnp.bfloat16

D_MODEL = 1024
DEPTH = 4
GRID_W = 64
ROPE_THETA = 10000.0
EPS = 1e-6
H_A, NOPE_A, ROPE_A, V_A = 8, 64, 32, 64
Q_RANK, KV_RANK = 384, 256
H_B, P_B, N_STATE, G_B, CHUNK = 16, 64, 128, 4, 128
D_INNER = H_B * P_B
CONV_CH = D_INNER + 2 * G_B * N_STATE
H_C, DH_C = 4, 64
DV_C = 2 * DH_C
D_FF = 2816
N_MOD = 6
COND_ROWS = 16

LOG2E = math.log2(math.e)
LANE = 128
SUBLANE = 8
VMEM_LIMIT = 56 * 1024 * 1024

C_GATE = 0
C_Z = 3072
C_XBC = 4096
C_QC = 6144
C_KC = 7168
C_VC = 7680
C_CQ = 8192
C_KROPE = 8576
C_CKV = 8704
C_DT = 8960
N_PROJ = 9216


def _cparams(*sem):
    return pltpu.CompilerParams(dimension_semantics=sem, vmem_limit_bytes=VMEM_LIMIT)


def _silu(x):
    return x * jax.nn.sigmoid(x)


def _rms(x, g):
    ms = jnp.mean(x * x, axis=-1, keepdims=True)
    return x * lax.rsqrt(ms + EPS) * g


def _rope(x, c, s):
    n = x.shape[-1]
    lane = lax.broadcasted_iota(jnp.int32, x.shape, x.ndim - 1)
    nxt = pltpu.roll(x, n - 1, x.ndim - 1)
    prv = pltpu.roll(x, 1, x.ndim - 1)
    return x * c + jnp.where(lane % 2 == 0, nxt, prv) * s


def _mod_kernel(c_ref, w_ref, b_ref, o_ref):
    s = _silu(c_ref[...]).astype(BF16)
    o_ref[0] = jnp.dot(s, w_ref[0].astype(BF16), preferred_element_type=F32) + b_ref[0]


def _mod_call(cond, w_mod, b_mod):
    tn = 1536
    n = N_MOD * D_MODEL
    return pl.pallas_call(
        _mod_kernel,
        grid=(DEPTH, n // tn),
        in_specs=[
            pl.BlockSpec((COND_ROWS, D_MODEL), lambda l, j: (0, 0)),
            pl.BlockSpec((1, D_MODEL, tn), lambda l, j: (l, 0, j)),
            pl.BlockSpec((1, 1, tn), lambda l, j: (l, 0, j)),
        ],
        out_specs=pl.BlockSpec((1, COND_ROWS, tn), lambda l, j: (l, 0, j)),
        out_shape=jax.ShapeDtypeStruct((DEPTH, COND_ROWS, n), F32),
        compiler_params=_cparams("parallel", "parallel"),
        name="mod",
    )(cond, w_mod, b_mod.reshape(DEPTH, 1, n))


def _inproj_kernel(x_ref, g_ref, mod_ref, w_ref, ws_ref, o_ref, os_ref, h_scr):
    @pl.when(pl.program_id(1) == 0)
    def _():
        h = _rms(x_ref[...], g_ref[...]) * (1.0 + mod_ref[0, 1:2, :]) + mod_ref[0, 0:1, :]
        hb = h.astype(BF16)
        h_scr[...] = hb
        os_ref[...] = jnp.dot(hb, ws_ref[...], preferred_element_type=F32)

    o_ref[...] = jnp.dot(h_scr[...], w_ref[...], preferred_element_type=F32).astype(BF16)


def _inproj_call(x2d, g, mod, w, w_side, row_fn, tm):
    t = x2d.shape[0]
    tn = 1024
    n_side = w_side.shape[1]
    return pl.pallas_call(
        _inproj_kernel,
        grid=(t // tm, N_PROJ // tn),
        in_specs=[
            pl.BlockSpec((tm, D_MODEL), lambda i, j: (i, 0)),
            pl.BlockSpec((1, D_MODEL), lambda i, j: (0, 0)),
            pl.BlockSpec((1, N_MOD, D_MODEL), lambda i, j: (row_fn(i, tm), 0, 0)),
            pl.BlockSpec((D_MODEL, tn), lambda i, j: (0, j)),
            pl.BlockSpec((D_MODEL, n_side), lambda i, j: (0, 0)),
        ],
        out_specs=[
            pl.BlockSpec((tm, tn), lambda i, j: (i, j)),
            pl.BlockSpec((tm, n_side), lambda i, j: (i, 0)),
        ],
        out_shape=[
            jax.ShapeDtypeStruct((t, N_PROJ), BF16),
            jax.ShapeDtypeStruct((t, n_side), F32),
        ],
        scratch_shapes=[pltpu.VMEM((tm, D_MODEL), BF16)],
        compiler_params=_cparams("parallel", "arbitrary"),
        name="inproj",
    )(x2d, g, mod, w, w_side)


def _mla_q_kernel(*refs, rope, scale):
    if rope:
        p_ref, g_ref, w_ref, c_ref, s_ref, o_ref = refs
    else:
        p_ref, g_ref, w_ref, o_ref = refs
    cqn = _rms(p_ref[:, :Q_RANK].astype(F32), g_ref[...]).astype(BF16)
    q = jnp.dot(cqn, w_ref[...], preferred_element_type=F32)
    for h in range(H_A):
        qh = q[:, h * LANE:(h + 1) * LANE]
        if rope:
            qh = _rope(qh, c_ref[...], s_ref[...])
        o_ref[:, h * LANE:(h + 1) * LANE] = (qh * scale).astype(BF16)


def _mla_q_call(proj, g, w, tabs, seq, tm):
    t = proj.shape[0]
    rope = tabs is not None
    nb = seq // tm
    in_specs = [
        pl.BlockSpec((tm, 512), lambda i: (i, C_CQ // 512)),
        pl.BlockSpec((1, Q_RANK), lambda i: (0, 0)),
        pl.BlockSpec((Q_RANK, H_A * LANE), lambda i: (0, 0)),
    ]
    args = [proj, g, w]
    if rope:
        in_specs += [pl.BlockSpec((tm, LANE), lambda i: (i % nb, 0))] * 2
        args += list(tabs)
    return pl.pallas_call(
        functools.partial(_mla_q_kernel, rope=rope, scale=LOG2E * (NOPE_A + ROPE_A) ** -0.5),
        grid=(t // tm,),
        in_specs=in_specs,
        out_specs=pl.BlockSpec((tm, H_A * LANE), lambda i: (i, 0)),
        out_shape=jax.ShapeDtypeStruct((t, H_A * LANE), BF16),
        compiler_params=_cparams("parallel"),
        name="mla_q",
    )(*args)


def _kv_up_kernel(*refs, norm, rope, want_ckv):
    refs = list(refs)
    ckv_ref, kr_ref = refs[0], refs[1]
    pos = 2
    if norm:
        g_ref = refs[pos]
        pos += 1
    w_ref = refs[pos]
    pos += 1
    if rope:
        c_ref, s_ref = refs[pos], refs[pos + 1]
        pos += 2
    k_ref, v_ref = refs[pos], refs[pos + 1]
    ckv = ckv_ref[...].astype(F32)
    if norm:
        ckv = _rms(ckv, g_ref[...])
        if want_ckv:
            refs[pos + 2][...] = ckv
    kv = jnp.dot(ckv.astype(BF16), w_ref[...], preferred_element_type=F32)
    kr = kr_ref[...].astype(F32)
    if rope:
        kr = _rope(kr, c_ref[...], s_ref[...])
    for h in range(H_A):
        k_ref[:, h * LANE:(h + 1) * LANE] = (kv[:, h * LANE:(h + 1) * LANE] + kr).astype(BF16)
    v_ref[...] = kv[:, H_A * LANE:].astype(BF16)


def _kv_up_call(ckv_src, ckv_col, kr_src, kr_col, g, w, tabs, seq, tm, want_ckv=False):
    t = ckv_src.shape[0]
    norm = g is not None
    rope = tabs is not None
    nb = seq // tm
    in_specs = [
        pl.BlockSpec((tm, KV_RANK), lambda i: (i, ckv_col)),
        pl.BlockSpec((tm, LANE), lambda i: (i, kr_col)),
    ]
    args = [ckv_src, kr_src]
    if norm:
        in_specs.append(pl.BlockSpec((1, KV_RANK), lambda i: (0, 0)))
        args.append(g)
    in_specs.append(pl.BlockSpec((KV_RANK, 2 * H_A * LANE), lambda i: (0, 0)))
    args.append(w)
    if rope:
        in_specs += [pl.BlockSpec((tm, LANE), lambda i: (i % nb, 0))] * 2
        args += list(tabs)
    out_specs = [pl.BlockSpec((tm, H_A * LANE), lambda i: (i, 0))] * 2
    out_shape = [jax.ShapeDtypeStruct((t, H_A * LANE), BF16)] * 2
    if want_ckv:
        out_specs.append(pl.BlockSpec((tm, KV_RANK), lambda i: (i, 0)))
        out_shape.append(jax.ShapeDtypeStruct((t, KV_RANK), F32))
    return pl.pallas_call(
        functools.partial(_kv_up_kernel, norm=norm, rope=rope, want_ckv=want_ckv),
        grid=(t // tm,),
        in_specs=in_specs,
        out_specs=out_specs,
        out_shape=out_shape,
        compiler_params=_cparams("parallel"),
        name="kv_up",
    )(*args)


def _dprep_kernel(*refs, rope):
    if rope:
        q_ref, k_ref, c_ref, s_ref, qo_ref, ko_ref = refs
    else:
        q_ref, k_ref, qo_ref, ko_ref = refs
    for b in range(2 * H_C):
        q = q_ref[:, b * LANE:(b + 1) * LANE].astype(F32)
        if rope:
            q = _rope(q, c_ref[...], s_ref[...])
        qo_ref[:, b * LANE:(b + 1) * LANE] = (q * (LOG2E * DH_C ** -0.5)).astype(BF16)
    for b in range(H_C):
        k = k_ref[:, b * LANE:(b + 1) * LANE].astype(F32)
        if rope:
            k = _rope(k, c_ref[...], s_ref[...])
        ko_ref[:, b * LANE:(b + 1) * LANE] = k.astype(BF16)


def _dprep_call(proj, tabs, seq, tm):
    t = proj.shape[0]
    rope = tabs is not None
    nb = seq // tm
    in_specs = [
        pl.BlockSpec((tm, 1024), lambda i: (i, C_QC // 1024)),
        pl.BlockSpec((tm, 512), lambda i: (i, C_KC // 512)),
    ]
    args = [proj, proj]
    if rope:
        in_specs += [pl.BlockSpec((tm, LANE), lambda i: (i % nb, 0))] * 2
        args += list(tabs)
    return pl.pallas_call(
        functools.partial(_dprep_kernel, rope=rope),
        grid=(t // tm,),
        in_specs=in_specs,
        out_specs=[
            pl.BlockSpec((tm, 1024), lambda i: (i, 0)),
            pl.BlockSpec((tm, 512), lambda i: (i, 0)),
        ],
        out_shape=[
            jax.ShapeDtypeStruct((t, 1024), BF16),
            jax.ShapeDtypeStruct((t, 512), BF16),
        ],
        compiler_params=_cparams("parallel"),
        name="dprep",
    )(*args)


def _attn_kernel(*refs, chunks, n_pieces, diff, tq, lam_init):
    q_ref = refs[0]
    kv = refs[1:1 + 2 * n_pieces]
    o_ref = refs[-1]
    q = q_ref[0]
    if diff:
        q = jnp.concatenate([q[:, :LANE], q[:, LANE:]], axis=0)
    nt = (((1,), (1,)), ((), ()))

    def lane_sum(p):
        parts = [p[:, c * LANE:(c + 1) * LANE] for c in range(p.shape[1] // LANE)]
        while len(parts) > 1:
            parts = [a + b for a, b in zip(parts[::2], parts[1::2])] + ([parts[-1]] if len(parts) % 2 else [])
        return parts[0]

    m = l = acc = None
    for pi, st, sz in chunks:
        k = kv[2 * pi][0, st:st + sz, :].astype(BF16)
        v = kv[2 * pi + 1][0, st:st + sz, :].astype(BF16)
        s = lax.dot_general(q, k, nt, preferred_element_type=F32)
        ms = s.max(axis=-1, keepdims=True)
        if m is None:
            m = ms
            p = jnp.exp2(s - m)
            l = lane_sum(p)
            acc = jnp.dot(p.astype(BF16), v, preferred_element_type=F32)
        else:
            m_new = jnp.maximum(m, ms)
            alpha = jnp.exp2(m - m_new)
            p = jnp.exp2(s - m_new)
            l = alpha * l + lane_sum(p)
            acc = alpha * acc + jnp.dot(p.astype(BF16), v, preferred_element_type=F32)
            m = m_new
    o = acc / l.sum(axis=-1, keepdims=True)
    if diff:
        lam_ref, g_ref = refs[1 + 2 * n_pieces], refs[2 + 2 * n_pieces]
        lv = lam_ref[...]
        lam = (jnp.exp(jnp.sum(lv[0:1] * lv[1:2], axis=-1, keepdims=True))
               - jnp.exp(jnp.sum(lv[2:3] * lv[3:4], axis=-1, keepdims=True)) + lam_init)
        oc = o[:tq] - lam * o[tq:]
        o = _rms(oc, g_ref[...]) * (1.0 - lam_init)
    o_ref[0] = o.astype(BF16)


KEY_CHUNK = 256


def _attn_call(q, pieces, n_heads, diff, tq, extra=(), lam_init=0.0):
    b, seq = q.shape[0], q.shape[1]
    qw = 2 * LANE if diff else LANE
    in_specs = [pl.BlockSpec((1, tq, qw), lambda bi, h, qi: (bi, qi, h))]
    args = [q]
    chunks = []
    for pi, (k, kc, v, vc) in enumerate(pieces):
        s = k.shape[1]
        in_specs += [pl.BlockSpec((1, s, LANE), lambda bi, h, qi, kc=kc: (bi, 0, kc + h)),
                     pl.BlockSpec((1, s, LANE), lambda bi, h, qi, vc=vc: (bi, 0, vc + h))]
        args += [k, v]
        step = _tile(s, KEY_CHUNK)
        chunks += [(pi, st, step) for st in range(0, s, step)]
    for e in extra:
        in_specs.append(pl.BlockSpec(e.shape, lambda bi, h, qi: (0, 0)))
        args.append(e)
    return pl.pallas_call(
        functools.partial(_attn_kernel, chunks=tuple(chunks), n_pieces=len(pieces), diff=diff, tq=tq,
                          lam_init=lam_init),
        grid=(b, n_heads, seq // tq),
        in_specs=in_specs,
        out_specs=pl.BlockSpec((1, tq, LANE), lambda bi, h, qi: (bi, qi, h)),
        out_shape=jax.ShapeDtypeStruct((b, seq, n_heads * LANE), BF16),
        compiler_params=_cparams("parallel", "parallel", "arbitrary"),
        name="diff_attn" if diff else "mla_attn",
    )(*args)


def _conv3(x, prev_ref, next_ref, w_ref, b_ref, tm, seq):
    i = pl.program_id(0)
    first = (i * tm) % seq == 0
    last = ((i + 1) * tm) % seq == 0
    pv = prev_ref[...].astype(F32)[prev_ref.shape[0] - 1:, :] * jnp.where(first, 0.0, 1.0)
    nx = next_ref[...].astype(F32)[0:1, :] * jnp.where(last, 0.0, 1.0)
    row = lax.broadcasted_iota(jnp.int32, x.shape, 0)
    xp = jnp.where(row == 0, pv, pltpu.roll(x, 1, 0))
    xn = jnp.where(row == tm - 1, nx, pltpu.roll(x, tm - 1, 0))
    return w_ref[0:1, :] * xp + w_ref[1:2, :] * x + w_ref[2:3, :] * xn + b_ref[...]


HALO = 16


def _halo_specs(tm, width, col, t):
    r = tm // HALO
    nblk = t // HALO
    return [
        pl.BlockSpec((tm, width), lambda i: (i, col)),
        pl.BlockSpec((HALO, width), lambda i: (jnp.maximum(i * r - 1, 0), col)),
        pl.BlockSpec((HALO, width), lambda i: (jnp.minimum((i + 1) * r, nblk - 1), col)),
    ]


def _ssd_conv_kernel(x_ref, prev_ref, next_ref, w_ref, b_ref, o_ref, *, tm, seq):
    o_ref[...] = _silu(_conv3(x_ref[...].astype(F32), prev_ref, next_ref, w_ref, b_ref, tm, seq))


def _ssd_conv_call(proj, w, b, seq, tm):
    t = proj.shape[0]
    return pl.pallas_call(
        functools.partial(_ssd_conv_kernel, tm=tm, seq=seq),
        grid=(t // tm,),
        in_specs=_halo_specs(tm, CONV_CH, C_XBC // CONV_CH, t) + [
            pl.BlockSpec((3, CONV_CH), lambda i: (0, 0)),
            pl.BlockSpec((1, CONV_CH), lambda i: (0, 0)),
        ],
        out_specs=pl.BlockSpec((tm, CONV_CH), lambda i: (i, 0)),
        out_shape=jax.ShapeDtypeStruct((t, CONV_CH), F32),
        compiler_params=_cparams("parallel"),
        name="ssd_conv",
    )(proj, proj, proj, w, b)


def _cumsum_rows(x, reverse):
    n = x.shape[0]
    row = lax.broadcasted_iota(jnp.int32, x.shape, 0)
    s = 1
    while s < n:
        if reverse:
            x = x + jnp.where(row < n - s, pltpu.roll(x, n - s, 0), 0.0)
        else:
            x = x + jnp.where(row >= s, pltpu.roll(x, s, 0), 0.0)
        s *= 2
    return x


def _ssd_kernel(*refs, nc, has_h0):
    if has_h0:
        xs_ref, bs_ref, cs_ref, dt_ref, dtb_ref, alog_ref, dsk_ref, h0_ref = refs[:8]
        y_ref, hfin_ref, h_scr, hb_scr = refs[8:]
    else:
        xs_ref, bs_ref, cs_ref, dt_ref, dtb_ref, alog_ref, dsk_ref = refs[:7]
        y_ref, hfin_ref, h_scr, hb_scr = refs[7:]
    ph = pl.program_id(1)
    k = pl.program_id(2)
    q = CHUNK
    hpg = H_B // G_B
    gw = hpg * P_B
    dtp = jax.nn.softplus(dt_ref[0] + dtb_ref[...])
    la = dtp * (-jnp.exp(alog_ref[...]))
    lane = lax.broadcasted_iota(jnp.int32, (q, LANE), 1)
    row = lax.broadcasted_iota(jnp.int32, (q, LANE), 0)
    nt_dims = (((1,), (1,)), ((), ()))

    def init_state(d):
        if has_h0:
            h_scr[...] = h0_ref[0, d]
        else:
            h_scr[...] = jnp.zeros_like(h_scr)

    def update_state(w, tot, off):
        w_t = w.T
        for g in range(G_B):
            bg = bs_ref[0, :, g * N_STATE:(g + 1) * N_STATE].astype(BF16)
            for pr in range(hpg // 2):
                hi = g * hpg + 2 * pr
                r0 = hi * P_B
                x_t = xs_ref[0, :, (hi // 2) * LANE:(hi // 2 + 1) * LANE].T
                w_rows = jnp.where(row < P_B, w_t[off + hi:off + hi + 1, :], w_t[off + hi + 1:off + hi + 2, :])
                st = jnp.dot((x_t * w_rows).astype(BF16), bg, preferred_element_type=F32)
                dec = jnp.where(row < P_B, jnp.exp(tot[:, off + hi:off + hi + 1]),
                                jnp.exp(tot[:, off + hi + 1:off + hi + 2]))
                h_scr[r0:r0 + 2 * P_B, :] = h_scr[r0:r0 + 2 * P_B, :] * dec + st

    @pl.when(ph == 0)
    def _():
        c = nc - 1 - k

        @pl.when(k == 0)
        def _():
            init_state(1)

        suf = _cumsum_rows(la, True)
        hb_scr[c] = h_scr[...]
        tot = suf[0:1, :]
        update_state(jnp.exp(tot - suf) * dtp, tot, H_B)

        @pl.when(k == nc - 1)
        def _():
            hfin_ref[0, 1] = h_scr[...]

    @pl.when(ph == 1)
    def _():
        @pl.when(k == 0)
        def _():
            init_state(0)

        pre = _cumsum_rows(la, False)
        suf = _cumsum_rows(la, True)
        pre_t = pre.T
        suf_t = suf.T
        dt_t = dtp.T
        lower = row >= lane
        upper = row <= lane
        hb = hb_scr[k]
        for g in range(G_B):
            cg = cs_ref[0, :, g * N_STATE:(g + 1) * N_STATE].astype(BF16)
            bg = bs_ref[0, :, g * N_STATE:(g + 1) * N_STATE].astype(BF16)
            cb = lax.dot_general(cg, bg, nt_dims, preferred_element_type=F32)
            hf_g = h_scr[g * gw:(g + 1) * gw, :].astype(BF16)
            hb_g = hb[g * gw:(g + 1) * gw, :].astype(BF16)
            y_f = lax.dot_general(cg, hf_g, nt_dims, preferred_element_type=F32)
            y_b = lax.dot_general(cg, hb_g, nt_dims, preferred_element_type=F32)
            for pr in range(hpg // 2):
                hi0 = g * hpg + 2 * pr
                cols = slice((hi0 // 2) * LANE, (hi0 // 2 + 1) * LANE)
                xp = xs_ref[0, :, cols]
                xpb = xp.astype(BF16)
                acc = None
                e_f, e_b = [], []
                for e in range(2):
                    hi = hi0 + e
                    hb_i = H_B + hi
                    col_p = jnp.broadcast_to(pre[:, hi:hi + 1], (q, LANE))
                    col_s = jnp.broadcast_to(suf[:, hb_i:hb_i + 1], (q, LANE))
                    lf = jnp.exp(jnp.where(lower, col_p - pre_t[hi:hi + 1, :], -jnp.inf))
                    lb = jnp.exp(jnp.where(upper, col_s - suf_t[hb_i:hb_i + 1, :], -jnp.inf))
                    mm = cb * (lf * dt_t[hi:hi + 1, :] + lb * dt_t[hb_i:hb_i + 1, :])
                    keep = (lane < P_B) if e == 0 else (lane >= P_B)
                    xm = jnp.where(keep, xpb, jnp.zeros_like(xpb))
                    d = jnp.dot(mm.astype(BF16), xm, preferred_element_type=F32)
                    acc = d if acc is None else acc + d
                    e_f.append(jnp.exp(col_p))
                    e_b.append(jnp.exp(col_s))
                lo = pr * LANE
                y = (acc
                     + y_f[:, lo:lo + LANE] * jnp.where(lane < P_B, e_f[0], e_f[1])
                     + y_b[:, lo:lo + LANE] * jnp.where(lane < P_B, e_b[0], e_b[1])
                     + dsk_ref[:, cols] * xp)
                y_ref[0, :, cols] = y
        tot = pre[q - 1:q, :]
        update_state(jnp.exp(tot - pre) * dtp, tot, 0)

        @pl.when(k == nc - 1)
        def _():
            hfin_ref[0, 0] = h_scr[...]


def _ssd_call(xbc, dt_src, dt_col, dtb, alog, dsk, h0):
    b, seq = xbc.shape[0], xbc.shape[1]
    nc = seq // CHUNK
    has_h0 = h0 is not None

    def cidx(ph, k):
        return jnp.where(ph == 0, nc - 1 - k, k)

    in_specs = [
        pl.BlockSpec((1, CHUNK, D_INNER), lambda bi, ph, k: (bi, cidx(ph, k), 0)),
        pl.BlockSpec((1, CHUNK, G_B * N_STATE), lambda bi, ph, k: (bi, cidx(ph, k), D_INNER // (G_B * N_STATE))),
        pl.BlockSpec((1, CHUNK, G_B * N_STATE), lambda bi, ph, k: (bi, cidx(ph, k), D_INNER // (G_B * N_STATE) + 1)),
        pl.BlockSpec((1, CHUNK, LANE), lambda bi, ph, k: (bi, cidx(ph, k), dt_col)),
        pl.BlockSpec((1, LANE), lambda bi, ph, k: (0, 0)),
        pl.BlockSpec((1, LANE), lambda bi, ph, k: (0, 0)),
        pl.BlockSpec((1, D_INNER), lambda bi, ph, k: (0, 0)),
    ]
    args = [xbc, xbc, xbc, dt_src, dtb, alog, dsk]
    if has_h0:
        in_specs.append(pl.BlockSpec((1, 2, D_INNER, N_STATE), lambda bi, ph, k: (bi, 0, 0, 0)))
        args.append(h0)
    return pl.pallas_call(
        functools.partial(_ssd_kernel, nc=nc, has_h0=has_h0),
        grid=(b, 2, nc),
        in_specs=in_specs,
        out_specs=[
            pl.BlockSpec((1, CHUNK, D_INNER), lambda bi, ph, k: (bi, ph * k, 0)),
            pl.BlockSpec((1, 2, D_INNER, N_STATE), lambda bi, ph, k: (bi, 0, 0, 0)),
        ],
        out_shape=[
            jax.ShapeDtypeStruct((b, seq, D_INNER), F32),
            jax.ShapeDtypeStruct((b, 2, D_INNER, N_STATE), F32),
        ],
        scratch_shapes=[
            pltpu.VMEM((D_INNER, N_STATE), F32),
            pltpu.VMEM((nc, D_INNER, N_STATE), F32),
        ],
        compiler_params=_cparams("parallel", "arbitrary", "arbitrary"),
        name="ssd",
    )(*args)


def _post_kernel(oa_ref, y_ref, z_ref, oc_ref, gate_ref, x_ref, mod_ref, gb_ref, g2_ref,
                 wa_ref, wb_ref, wc_ref, wo_ref, xo_ref, h2_ref):
    yb = _rms(y_ref[...] * _silu(z_ref[...].astype(F32)), gb_ref[...]).astype(BF16)
    out_a = jnp.dot(oa_ref[...], wa_ref[...], preferred_element_type=F32)
    out_b = jnp.dot(yb, wb_ref[...], preferred_element_type=F32)
    out_c = jnp.dot(oc_ref[...], wc_ref[...], preferred_element_type=F32)
    merged = (jax.nn.sigmoid(gate_ref[:, 0:D_MODEL].astype(F32)) * out_a
              + jax.nn.sigmoid(gate_ref[:, D_MODEL:2 * D_MODEL].astype(F32)) * out_b
              + jax.nn.sigmoid(gate_ref[:, 2 * D_MODEL:3 * D_MODEL].astype(F32)) * out_c)
    out = jnp.dot(merged.astype(BF16), wo_ref[...], preferred_element_type=F32)
    xn = x_ref[...] + mod_ref[0, 2:3, :] * out
    xo_ref[...] = xn
    h2 = _rms(xn, g2_ref[...]) * (1.0 + mod_ref[0, 4:5, :]) + mod_ref[0, 3:4, :]
    h2_ref[...] = h2.astype(BF16)


def _post_call(oa, y, proj, oc, x2d, mod, gb, g2, wa, wb, wc, wo, row_fn, tm):
    t = x2d.shape[0]
    full = lambda i: (0, 0)
    return pl.pallas_call(
        _post_kernel,
        grid=(t // tm,),
        in_specs=[
            pl.BlockSpec((tm, H_A * LANE), lambda i: (i, 0)),
            pl.BlockSpec((tm, D_INNER), lambda i: (i, 0)),
            pl.BlockSpec((tm, D_INNER), lambda i: (i, C_Z // D_INNER)),
            pl.BlockSpec((tm, H_C * DV_C), lambda i: (i, 0)),
            pl.BlockSpec((tm, 3 * D_MODEL), lambda i: (i, 0)),
            pl.BlockSpec((tm, D_MODEL), lambda i: (i, 0)),
            pl.BlockSpec((1, N_MOD, D_MODEL), lambda i: (row_fn(i, tm), 0, 0)),
            pl.BlockSpec((1, D_INNER), full),
            pl.BlockSpec((1, D_MODEL), full),
            pl.BlockSpec((H_A * LANE, D_MODEL), full),
            pl.BlockSpec((D_INNER, D_MODEL), full),
            pl.BlockSpec((H_C * DV_C, D_MODEL), full),
            pl.BlockSpec((D_MODEL, D_MODEL), full),
        ],
        out_specs=[
            pl.BlockSpec((tm, D_MODEL), lambda i: (i, 0)),
            pl.BlockSpec((tm, D_MODEL), lambda i: (i, 0)),
        ],
        out_shape=[
            jax.ShapeDtypeStruct((t, D_MODEL), F32),
            jax.ShapeDtypeStruct((t, D_MODEL), BF16),
        ],
        compiler_params=_cparams("parallel"),
        name="post_mix",
    )(oa, y, proj, oc, proj, x2d, mod, gb, g2, wa, wb, wc, wo)


FFN_CHUNK = 256


def _ffn_kernel(h_ref, prev_ref, next_ref, wu_ref, cw_ref, cb_ref, wd_ref, x_ref, mod_ref, o_ref, a_scr, *, tm, seq):
    i = pl.program_id(0)
    hext = jnp.concatenate([prev_ref[...], h_ref[...], next_ref[...]], axis=0)
    pos = (i * tm + lax.broadcasted_iota(jnp.int32, (tm, 1), 0)) % seq
    has_prev = pos != 0
    has_next = pos != seq - 1
    ext = tm + 2 * HALO

    def conv(u, c0):
        up = pltpu.roll(u, 1, 0)[HALO:HALO + tm]
        un = pltpu.roll(u, ext - 1, 0)[HALO:HALO + tm]
        cols = slice(c0, c0 + FFN_CHUNK)
        return (cw_ref[0:1, cols] * jnp.where(has_prev, up, 0.0) + cw_ref[1:2, cols] * u[HALO:HALO + tm]
                + cw_ref[2:3, cols] * jnp.where(has_next, un, 0.0) + cb_ref[:, cols])

    for c in range(D_FF // FFN_CHUNK):
        g0, v0 = c * FFN_CHUNK, D_FF + c * FFN_CHUNK
        ug = jnp.dot(hext, wu_ref[:, g0:g0 + FFN_CHUNK], preferred_element_type=F32)
        uv = jnp.dot(hext, wu_ref[:, v0:v0 + FFN_CHUNK], preferred_element_type=F32)
        a_scr[:, g0:g0 + FFN_CHUNK] = (_silu(conv(ug, g0)) * conv(uv, v0)).astype(BF16)
    out = jnp.dot(a_scr[...], wd_ref[...], preferred_element_type=F32)
    o_ref[...] = x_ref[...] + mod_ref[0, 5:6, :] * out


def _ffn_call(h2, wu, cw, cb, wd, x2d, mod, row_fn, seq, tm):
    t = h2.shape[0]
    n = 2 * D_FF
    full = lambda i: (0, 0)
    return pl.pallas_call(
        functools.partial(_ffn_kernel, tm=tm, seq=seq),
        grid=(t // tm,),
        in_specs=_halo_specs(tm, D_MODEL, 0, t) + [
            pl.BlockSpec((D_MODEL, n), full),
            pl.BlockSpec((3, n), full),
            pl.BlockSpec((1, n), full),
            pl.BlockSpec((D_FF, D_MODEL), full),
            pl.BlockSpec((tm, D_MODEL), lambda i: (i, 0)),
            pl.BlockSpec((1, N_MOD, D_MODEL), lambda i: (row_fn(i, tm), 0, 0)),
        ],
        out_specs=pl.BlockSpec((tm, D_MODEL), lambda i: (i, 0)),
        out_shape=jax.ShapeDtypeStruct((t, D_MODEL), F32),
        scratch_shapes=[pltpu.VMEM((tm, D_FF), BF16)],
        compiler_params=_cparams("parallel"),
        name="ffn",
    )(h2, h2, h2, wu, cw, cb, wd, x2d, mod)


def _final_norm_kernel(x_ref, g_ref, o_ref):
    o_ref[...] = _rms(x_ref[...], g_ref[...])


def _final_norm_call(x2d, g, tm):
    t = x2d.shape[0]
    return pl.pallas_call(
        _final_norm_kernel,
        grid=(t // tm,),
        in_specs=[pl.BlockSpec((tm, D_MODEL), lambda i: (i, 0)), pl.BlockSpec((1, D_MODEL), lambda i: (0, 0))],
        out_specs=pl.BlockSpec((tm, D_MODEL), lambda i: (i, 0)),
        out_shape=jax.ShapeDtypeStruct((t, D_MODEL), F32),
        compiler_params=_cparams("parallel"),
        name="final_norm",
    )(x2d, g)


def _rope_tables(n_tokens, rot_dim, lead, tail):
    rows = n_tokens // GRID_W
    row = jnp.broadcast_to(jnp.arange(rows, dtype=F32)[:, None], (rows, GRID_W)).reshape(-1)
    col = jnp.broadcast_to(jnp.arange(GRID_W, dtype=F32)[None, :], (rows, GRID_W)).reshape(-1)
    n_freq = rot_dim // 4
    freqs = ROPE_THETA ** (-jnp.arange(n_freq, dtype=F32) / n_freq)
    ang = jnp.concatenate([row[:, None] * freqs, col[:, None] * freqs], axis=-1)
    cos, sin = jnp.cos(ang), jnp.sin(ang)
    c = jnp.repeat(cos, 2, axis=-1)
    s = jnp.stack([-sin, sin], axis=-1).reshape(n_tokens, rot_dim)
    reps = (LANE - lead - tail) // rot_dim
    c = jnp.tile(c, (1, reps))
    s = jnp.tile(s, (1, reps))
    c = jnp.concatenate([jnp.ones((n_tokens, lead), F32), c, jnp.ones((n_tokens, tail), F32)], axis=-1)
    s = jnp.concatenate([jnp.zeros((n_tokens, lead), F32), s, jnp.zeros((n_tokens, tail), F32)], axis=-1)
    return c, s


def _pad_w_in(w):
    cq, ckv, krope, z, xbc, dt, qc, kc, vc, gate = jnp.split(
        w, [384, 640, 672, 1696, 3744, 3776, 4288, 4800, 5312], axis=-1)
    d = w.shape[0]
    qc4 = qc.reshape(d, H_C, 2, DH_C)
    zq = jnp.zeros((d, H_C, DH_C), w.dtype)
    qc_pad = jnp.stack([jnp.concatenate([qc4[:, :, 0], zq], -1), jnp.concatenate([zq, qc4[:, :, 1]], -1)],
                       axis=2).reshape(d, 2 * H_C * LANE)
    zeros = lambda n: jnp.zeros((d, n), w.dtype)
    kr_pad = jnp.concatenate([zeros(NOPE_A), krope, zeros(LANE - NOPE_A - ROPE_A)], axis=-1)
    dt_pad = jnp.concatenate([dt, zeros(LANE - 2 * H_B)], axis=-1)
    main = jnp.concatenate([gate, z, xbc, qc_pad, kc, vc, cq, kr_pad, ckv, dt_pad, zeros(LANE)], axis=-1)
    side_ctx = jnp.concatenate([kc, vc, ckv, kr_pad, dt_pad], axis=-1)
    return main.astype(BF16), side_ctx.astype(BF16), dt_pad.astype(BF16)


S_KC, S_VC, S_CKV, S_KROPE, S_DT = 0, 512, 1024, 1280, 1408


def _prep_layer(l, w):
    d_q = NOPE_A + ROPE_A
    w_uq = w['w_uq'][l].reshape(Q_RANK, H_A, d_q)
    w_uq = jnp.pad(w_uq, ((0, 0), (0, 0), (0, LANE - d_q))).reshape(Q_RANK, H_A * LANE)
    w_ukv = w['w_ukv'][l].reshape(KV_RANK, H_A, NOPE_A + V_A)
    wk = jnp.pad(w_ukv[:, :, :NOPE_A], ((0, 0), (0, 0), (0, LANE - NOPE_A))).reshape(KV_RANK, H_A * LANE)
    wv = jnp.pad(w_ukv[:, :, NOPE_A:], ((0, 0), (0, 0), (0, LANE - V_A))).reshape(KV_RANK, H_A * LANE)
    w_oa = jnp.pad(w['w_oA'][l].reshape(H_A, V_A, D_MODEL), ((0, 0), (0, LANE - V_A), (0, 0)))
    pad_lane = lambda v: jnp.pad(v.reshape(1, -1), ((0, 0), (0, LANE - v.size)))
    w_in, w_side_ctx, w_side_lat = _pad_w_in(w['w_in'][l])
    return dict(
        w_in=w_in, w_side_ctx=w_side_ctx, w_side_lat=w_side_lat,
        norm1_g=w['norm1_g'][l].reshape(1, -1),
        norm2_g=w['norm2_g'][l].reshape(1, -1),
        q_norm_g=w['q_norm_g'][l].reshape(1, -1),
        kv_norm_g=w['kv_norm_g'][l].reshape(1, -1),
        w_uq=w_uq.astype(BF16),
        w_ukv=jnp.concatenate([wk, wv], axis=-1).astype(BF16),
        w_oA=w_oa.reshape(H_A * LANE, D_MODEL).astype(BF16),
        w_oB=w['w_oB'][l].astype(BF16),
        w_oC=w['w_oC'][l].astype(BF16),
        w_out=w['w_out'][l].astype(BF16),
        ssd_conv_w=w['ssd_conv_w'][l],
        ssd_conv_b=w['ssd_conv_b'][l].reshape(1, -1),
        dt_bias=pad_lane(w['ssd_dt_bias'][l]),
        a_log=pad_lane(w['ssd_a_log'][l]),
        d_skip=jnp.repeat(w['ssd_d_skip'][l], P_B).reshape(1, -1),
        ssd_norm_g=w['ssd_norm_g'][l].reshape(1, -1),
        diff_lambda=w['diff_lambda'][l],
        diff_norm_g=w['diff_norm_g'][l].reshape(1, -1),
        ffn_w_up=w['ffn_w_up'][l].astype(BF16),
        ffn_conv_w=w['ffn_conv_w'][l],
        ffn_conv_b=w['ffn_conv_b'][l].reshape(1, -1),
        ffn_w_down=w['ffn_w_down'][l].astype(BF16),
    )


def _tile(n, pref):
    t = min(n, pref)
    while n % t:
        t //= 2
    return t


def _block(x, mod, lp, lam_init, row_fn, tabs, ctx):
    b, seq, _ = x.shape
    t = b * seq
    x2d = x.reshape(t, D_MODEL)
    is_ctx = ctx is None
    tm_big = _tile(t if is_ctx else seq, 1024)
    tm = _tile(seq, 256)
    proj, side = _inproj_call(x2d, lp['norm1_g'], mod, lp['w_in'],
                              lp['w_side_ctx'] if is_ctx else lp['w_side_lat'], row_fn, tm_big)
    proj3 = proj.reshape(b, seq, N_PROJ)
    tab_a, tab_c = tabs if tabs is not None else (None, None)
    qa = _mla_q_call(proj, lp['q_norm_g'], lp['w_uq'], tab_a, seq, tm_big)
    if is_ctx:
        k_new, v_new, ckv_n = _kv_up_call(side, S_CKV // KV_RANK, side, S_KROPE // LANE, lp['kv_norm_g'],
                                          lp['w_ukv'], None, seq, tm_big, want_ckv=True)
    else:
        k_new, v_new = _kv_up_call(proj, C_CKV // KV_RANK, proj, C_KROPE // LANE, lp['kv_norm_g'], lp['w_ukv'],
                                   tab_a, seq, tm_big)
    wide = H_A * LANE
    pieces_a = [(k_new.reshape(b, seq, wide), 0, v_new.reshape(b, seq, wide), 0)]
    qd, kd = _dprep_call(proj, tab_c, seq, tm_big)
    pieces_c = [(kd.reshape(b, seq, H_C * LANE), 0, proj3, C_VC // LANE)]
    h0 = None
    if not is_ctx:
        ctx_ckv, ctx_krope, ctx_k, ctx_v, h0 = ctx
        past = ctx_ckv.shape[1]
        kr_pad = jnp.pad(ctx_krope.reshape(b * past, ROPE_A), ((0, 0), (NOPE_A, LANE - NOPE_A - ROPE_A)))
        k_c, v_c = _kv_up_call(ctx_ckv.reshape(b * past, KV_RANK), 0, kr_pad, 0, None, lp['w_ukv'],
                               None, past, _tile(b * past, 1024))
        pieces_a = [(k_c.reshape(b, past, wide), 0, v_c.reshape(b, past, wide), 0)] + pieces_a
        pieces_c = [(ctx_k.reshape(b, past, H_C * LANE), 0, ctx_v.reshape(b, past, H_C * DV_C), 0)] + pieces_c
        h0 = h0.reshape(b, 2, D_INNER, N_STATE)
    oa = _attn_call(qa.reshape(b, seq, wide), pieces_a, H_A, False, _tile(seq, 1024))
    oc = _attn_call(qd.reshape(b, seq, 2 * H_C * LANE), pieces_c, H_C, True, _tile(seq, 512),
                    extra=(lp['diff_lambda'], lp['diff_norm_g']), lam_init=lam_init)
    xbc = _ssd_conv_call(proj, lp['ssd_conv_w'], lp['ssd_conv_b'], seq, tm)
    y, hfin = _ssd_call(xbc.reshape(b, seq, CONV_CH), side.reshape(b, seq, side.shape[1]),
                        S_DT // LANE if is_ctx else 0, lp['dt_bias'], lp['a_log'], lp['d_skip'], h0)
    x2d, h2 = _post_call(oa.reshape(t, wide), y.reshape(t, D_INNER), proj, oc.reshape(t, H_C * DV_C), x2d, mod,
                         lp['ssd_norm_g'], lp['norm2_g'], lp['w_oA'], lp['w_oB'], lp['w_oC'], lp['w_out'],
                         row_fn, tm)
    x2d = _ffn_call(h2, lp['ffn_w_up'], lp['ffn_conv_w'], lp['ffn_conv_b'], lp['ffn_w_down'], x2d, mod, row_fn,
                    seq, _tile(t if is_ctx else seq, 512))
    st = None
    if is_ctx:
        st = (ckv_n.reshape(b, seq, KV_RANK),
              side[:, S_KROPE + NOPE_A:S_KROPE + NOPE_A + ROPE_A].reshape(b, seq, ROPE_A),
              side[:, S_KC:S_KC + H_C * 2 * DH_C].reshape(b, seq, H_C, 2, DH_C),
              side[:, S_VC:S_VC + H_C * DV_C].reshape(b, seq, H_C, DV_C),
              hfin.reshape(b, 2, H_B, P_B, N_STATE))
    return x2d.reshape(b, seq, D_MODEL), st


def kernel(x_prompt, x_sample, cache_mla_ckv, cache_mla_krope, cache_diff_k, cache_diff_v, state_ssd, c, c_ctx, w_mod, b_mod, norm1_g, norm2_g, w_in, q_norm_g, kv_norm_g, w_uq, w_ukv, w_oA, ssd_conv_w, ssd_conv_b, ssd_dt_bias, ssd_a_log, ssd_d_skip, ssd_norm_g, w_oB, diff_lambda, diff_norm_g, w_oC, w_out, ffn_w_up, ffn_conv_w, ffn_conv_b, ffn_w_down, final_norm_g):
    weights = dict(norm1_g=norm1_g, norm2_g=norm2_g, w_in=w_in, q_norm_g=q_norm_g, kv_norm_g=kv_norm_g,
                   w_uq=w_uq, w_ukv=w_ukv, w_oA=w_oA, ssd_conv_w=ssd_conv_w, ssd_conv_b=ssd_conv_b,
                   ssd_dt_bias=ssd_dt_bias, ssd_a_log=ssd_a_log, ssd_d_skip=ssd_d_skip, ssd_norm_g=ssd_norm_g,
                   w_oB=w_oB, diff_lambda=diff_lambda, diff_norm_g=diff_norm_g, w_oC=w_oC, w_out=w_out,
                   ffn_w_up=ffn_w_up, ffn_conv_w=ffn_conv_w, ffn_conv_b=ffn_conv_b, ffn_w_down=ffn_w_down)
    n_dec, n_lat = x_sample.shape[0], x_sample.shape[1]
    ctx_row = n_dec
    cond = jnp.concatenate([c, c_ctx[None, :], jnp.zeros((COND_ROWS - n_dec - 1, D_MODEL), F32)], axis=0)
    mod_all = _mod_call(cond, w_mod, b_mod).reshape(DEPTH, COND_ROWS, N_MOD, D_MODEL)
    tabs = (_rope_tables(n_lat, ROPE_A, NOPE_A, LANE - NOPE_A - ROPE_A), _rope_tables(n_lat, DH_C, 0, 0))
    lat_row = lambda i, tm: (i * tm) // n_lat
    ctx_row_fn = lambda i, tm: ctx_row
    xp, xs = x_prompt, x_sample
    new = [[] for _ in range(5)]
    for l in range(DEPTH):
        lp = _prep_layer(l, weights)
        lam_init = 0.8 - 0.6 * math.exp(-0.3 * l)
        xp, st = _block(xp, mod_all[l], lp, lam_init, ctx_row_fn, None, None)
        for acc, v in zip(new, st):
            acc.append(v)
        ctx = (cache_mla_ckv[:, l], cache_mla_krope[:, l], cache_diff_k[:, l], cache_diff_v[:, l], state_ssd[:, l])
        xs, _ = _block(xs, mod_all[l], lp, lam_init, lat_row, tabs, ctx)
    bp, sp = xp.shape[0], xp.shape[1]
    y_prompt = _final_norm_call(xp.reshape(bp * sp, D_MODEL), final_norm_g.reshape(1, -1), _tile(bp * sp, 1024))
    y_sample = _final_norm_call(xs.reshape(n_dec * n_lat, D_MODEL), final_norm_g.reshape(1, -1),
                                _tile(n_dec * n_lat, 1024))
    return (y_prompt.reshape(bp, sp, D_MODEL), y_sample.reshape(n_dec, n_lat, D_MODEL),
            jnp.stack(new[0], axis=1), jnp.stack(new[1], axis=1), jnp.stack(new[2], axis=1),
            jnp.stack(new[3], axis=1), jnp.stack(new[4], axis=1))
```

```python
import functools
import math

import jax
import jax.numpy as jnp
from jax import lax
from jax.experimental import pallas as pl
from jax.experimental.pallas import tpu as pltpu

F32 = jnp.float32
BF16 = jnp.bfloat16

D_MODEL = 1024
DEPTH = 4
GRID_W = 64
ROPE_THETA = 10000.0
EPS = 1e-6
H_A, NOPE_A, ROPE_A, V_A = 8, 64, 32, 64
Q_RANK, KV_RANK = 384, 256
H_B, P_B, N_STATE, G_B, CHUNK = 16, 64, 128, 4, 128
D_INNER = H_B * P_B
CONV_CH = D_INNER + 2 * G_B * N_STATE
H_C, DH_C = 4, 64
DV_C = 2 * DH_C
D_FF = 2816
N_MOD = 6
COND_ROWS = 16

LOG2E = math.log2(math.e)
LANE = 128
SUBLANE = 8
VMEM_LIMIT = 56 * 1024 * 1024

C_GATE = 0
C_Z = 3072
C_XBC = 4096
C_QC = 6144
C_KC = 6656
C_VC = 7168
C_CQ = 7680
C_KROPE = 8064
C_CKV = 8192
N_PROJ = 8448
PROJ_TILE = 2816


def _cparams(*sem):
    return pltpu.CompilerParams(dimension_semantics=sem, vmem_limit_bytes=VMEM_LIMIT)


def _silu(x):
    return x * jax.nn.sigmoid(x)


def _rms(x, g):
    ms = jnp.mean(x * x, axis=-1, keepdims=True)
    return x * lax.rsqrt(ms + EPS) * g


def _rope(x, c, s):
    n = x.shape[-1]
    lane = lax.broadcasted_iota(jnp.int32, x.shape, x.ndim - 1)
    nxt = pltpu.roll(x, n - 1, x.ndim - 1)
    prv = pltpu.roll(x, 1, x.ndim - 1)
    return x * c + jnp.where(lane % 2 == 0, nxt, prv) * s


def _mod_kernel(c_ref, w_ref, b_ref, o_ref):
    s = _silu(c_ref[...]).astype(BF16)
    o_ref[0] = jnp.dot(s, w_ref[0].astype(BF16), preferred_element_type=F32) + b_ref[0]


def _mod_call(cond, w_mod, b_mod):
    tn = 1536
    n = N_MOD * D_MODEL
    return pl.pallas_call(
        _mod_kernel,
        grid=(DEPTH, n // tn),
        in_specs=[
            pl.BlockSpec((COND_ROWS, D_MODEL), lambda l, j: (0, 0)),
            pl.BlockSpec((1, D_MODEL, tn), lambda l, j: (l, 0, j)),
            pl.BlockSpec((1, 1, tn), lambda l, j: (l, 0, j)),
        ],
        out_specs=pl.BlockSpec((1, COND_ROWS, tn), lambda l, j: (l, 0, j)),
        out_shape=jax.ShapeDtypeStruct((DEPTH, COND_ROWS, n), F32),
        compiler_params=_cparams("parallel", "parallel"),
        name="mod",
    )(cond, w_mod, b_mod.reshape(DEPTH, 1, n))


def _inproj_kernel(x_ref, g_ref, mod_ref, w_ref, ws_ref, o_ref, os_ref, h_scr):
    @pl.when(pl.program_id(1) == 0)
    def _():
        h = _rms(x_ref[...], g_ref[...]) * (1.0 + mod_ref[0, 1:2, :]) + mod_ref[0, 0:1, :]
        hb = h.astype(BF16)
        h_scr[...] = hb
        os_ref[...] = jnp.dot(hb, ws_ref[...], preferred_element_type=F32)

    o_ref[...] = jnp.dot(h_scr[...], w_ref[...], preferred_element_type=F32).astype(BF16)


def _inproj_call(x2d, g, mod, w, w_side, row_fn, tm):
    t = x2d.shape[0]
    tn = PROJ_TILE
    n_side = w_side.shape[1]
    return pl.pallas_call(
        _inproj_kernel,
        grid=(t // tm, N_PROJ // tn),
        in_specs=[
            pl.BlockSpec((tm, D_MODEL), lambda i, j: (i, 0)),
            pl.BlockSpec((1, D_MODEL), lambda i, j: (0, 0)),
            pl.BlockSpec((1, N_MOD, D_MODEL), lambda i, j: (row_fn(i, tm), 0, 0)),
            pl.BlockSpec((D_MODEL, tn), lambda i, j: (0, j)),
            pl.BlockSpec((D_MODEL, n_side), lambda i, j: (0, 0)),
        ],
        out_specs=[
            pl.BlockSpec((tm, tn), lambda i, j: (i, j)),
            pl.BlockSpec((tm, n_side), lambda i, j: (i, 0)),
        ],
        out_shape=[
            jax.ShapeDtypeStruct((t, N_PROJ), BF16),
            jax.ShapeDtypeStruct((t, n_side), F32),
        ],
        scratch_shapes=[pltpu.VMEM((tm, D_MODEL), BF16)],
        compiler_params=_cparams("parallel", "arbitrary"),
        name="inproj",
    )(x2d, g, mod, w, w_side)


def _mla_q_kernel(*refs, rope, scale):
    if rope:
        p_ref, g_ref, w_ref, c_ref, s_ref, o_ref = refs
    else:
        p_ref, g_ref, w_ref, o_ref = refs
    cqn = _rms(p_ref[:, :Q_RANK].astype(F32), g_ref[...]).astype(BF16)
    q = jnp.dot(cqn, w_ref[...], preferred_element_type=F32)
    for h in range(H_A):
        qh = q[:, h * LANE:(h + 1) * LANE]
        if rope:
            qh = _rope(qh, c_ref[...], s_ref[...])
        o_ref[:, h * LANE:(h + 1) * LANE] = (qh * scale).astype(BF16)


def _mla_q_call(proj, g, w, tabs, seq, tm):
    t = proj.shape[0]
    rope = tabs is not None
    nb = seq // tm
    in_specs = [
        pl.BlockSpec((tm, 512), lambda i: (i, C_CQ // 512)),
        pl.BlockSpec((1, Q_RANK), lambda i: (0, 0)),
        pl.BlockSpec((Q_RANK, H_A * LANE), lambda i: (0, 0)),
    ]
    args = [proj, g, w]
    if rope:
        in_specs += [pl.BlockSpec((tm, LANE), lambda i: (i % nb, 0))] * 2
        args += list(tabs)
    return pl.pallas_call(
        functools.partial(_mla_q_kernel, rope=rope, scale=LOG2E * (NOPE_A + ROPE_A) ** -0.5),
        grid=(t // tm,),
        in_specs=in_specs,
        out_specs=pl.BlockSpec((tm, H_A * LANE), lambda i: (i, 0)),
        out_shape=jax.ShapeDtypeStruct((t, H_A * LANE), BF16),
        compiler_params=_cparams("parallel"),
        name="mla_q",
    )(*args)


def _kv_up_kernel(*refs, norm, rope, want_ckv):
    refs = list(refs)
    ckv_ref, kr_ref = refs[0], refs[1]
    pos = 2
    if norm:
        g_ref = refs[pos]
        pos += 1
    w_ref = refs[pos]
    pos += 1
    if rope:
        c_ref, s_ref = refs[pos], refs[pos + 1]
        pos += 2
    k_ref, v_ref = refs[pos], refs[pos + 1]
    ckv = ckv_ref[...].astype(F32)
    if norm:
        ckv = _rms(ckv, g_ref[...])
        if want_ckv:
            refs[pos + 2][...] = ckv
    kv = jnp.dot(ckv.astype(BF16), w_ref[...], preferred_element_type=F32)
    kr = kr_ref[...].astype(F32)
    if rope:
        kr = _rope(kr, c_ref[...], s_ref[...])
    for h in range(H_A):
        k_ref[:, h * LANE:(h + 1) * LANE] = (kv[:, h * LANE:(h + 1) * LANE] + kr).astype(BF16)
    v_ref[...] = kv[:, H_A * LANE:].astype(BF16)


def _kv_up_call(ckv_src, ckv_col, kr_src, kr_col, g, w, tabs, seq, tm, want_ckv=False):
    t = ckv_src.shape[0]
    norm = g is not None
    rope = tabs is not None
    nb = seq // tm
    in_specs = [
        pl.BlockSpec((tm, KV_RANK), lambda i: (i, ckv_col)),
        pl.BlockSpec((tm, LANE), lambda i: (i, kr_col)),
    ]
    args = [ckv_src, kr_src]
    if norm:
        in_specs.append(pl.BlockSpec((1, KV_RANK), lambda i: (0, 0)))
        args.append(g)
    in_specs.append(pl.BlockSpec((KV_RANK, 2 * H_A * LANE), lambda i: (0, 0)))
    args.append(w)
    if rope:
        in_specs += [pl.BlockSpec((tm, LANE), lambda i: (i % nb, 0))] * 2
        args += list(tabs)
    out_specs = [pl.BlockSpec((tm, H_A * LANE), lambda i: (i, 0))] * 2
    out_shape = [jax.ShapeDtypeStruct((t, H_A * LANE), BF16)] * 2
    if want_ckv:
        out_specs.append(pl.BlockSpec((tm, KV_RANK), lambda i: (i, 0)))
        out_shape.append(jax.ShapeDtypeStruct((t, KV_RANK), F32))
    return pl.pallas_call(
        functools.partial(_kv_up_kernel, norm=norm, rope=rope, want_ckv=want_ckv),
        grid=(t // tm,),
        in_specs=in_specs,
        out_specs=out_specs,
        out_shape=out_shape,
        compiler_params=_cparams("parallel"),
        name="kv_up",
    )(*args)


def _dprep_kernel(*refs, rope):
    if rope:
        q_ref, k_ref, c_ref, s_ref, qo_ref, ko_ref = refs
    else:
        q_ref, k_ref, qo_ref, ko_ref = refs
    for b in range(H_C):
        q = q_ref[:, b * LANE:(b + 1) * LANE].astype(F32)
        if rope:
            q = _rope(q, c_ref[...], s_ref[...])
        qo_ref[:, b * LANE:(b + 1) * LANE] = (q * (LOG2E * DH_C ** -0.5)).astype(BF16)
    for b in range(H_C):
        k = k_ref[:, b * LANE:(b + 1) * LANE].astype(F32)
        if rope:
            k = _rope(k, c_ref[...], s_ref[...])
        ko_ref[:, b * LANE:(b + 1) * LANE] = k.astype(BF16)


def _dprep_call(proj, tabs, seq, tm):
    t = proj.shape[0]
    rope = tabs is not None
    nb = seq // tm
    in_specs = [
        pl.BlockSpec((tm, 512), lambda i: (i, C_QC // 512)),
        pl.BlockSpec((tm, 512), lambda i: (i, C_KC // 512)),
    ]
    args = [proj, proj]
    if rope:
        in_specs += [pl.BlockSpec((tm, LANE), lambda i: (i % nb, 0))] * 2
        args += list(tabs)
    return pl.pallas_call(
        functools.partial(_dprep_kernel, rope=rope),
        grid=(t // tm,),
        in_specs=in_specs,
        out_specs=[
            pl.BlockSpec((tm, 512), lambda i: (i, 0)),
            pl.BlockSpec((tm, 512), lambda i: (i, 0)),
        ],
        out_shape=[
            jax.ShapeDtypeStruct((t, 512), BF16),
            jax.ShapeDtypeStruct((t, 512), BF16),
        ],
        compiler_params=_cparams("parallel"),
        name="dprep",
    )(*args)


def _attn_kernel(*refs, chunks, n_pieces, diff, tq, hp, lam_init):
    q_ref = refs[0]
    kv = refs[1:1 + 2 * n_pieces]
    o_ref = refs[-1]
    nt = (((1,), (1,)), ((), ()))

    def lane_sum(p):
        parts = [p[:, c * LANE:(c + 1) * LANE] for c in range(p.shape[1] // LANE)]
        while len(parts) > 1:
            parts = [a + b for a, b in zip(parts[::2], parts[1::2])] + ([parts[-1]] if len(parts) % 2 else [])
        return parts[0]

    def one_head(h):
        cols = slice(h * LANE, (h + 1) * LANE)
        q = q_ref[0, :, cols]
        if diff:
            lane = lax.broadcasted_iota(jnp.int32, q.shape, 1)
            zero = jnp.zeros_like(q)
            q = jnp.concatenate([jnp.where(lane < DH_C, q, zero), jnp.where(lane >= DH_C, q, zero)], axis=0)
        m = l = acc = None
        for pi, st, sz in chunks:
            k = kv[2 * pi][0, st:st + sz, cols].astype(BF16)
            v = kv[2 * pi + 1][0, st:st + sz, cols].astype(BF16)
            s = lax.dot_general(q, k, nt, preferred_element_type=F32)
            ms = s.max(axis=-1, keepdims=True)
            if m is None:
                m = ms
                p = jnp.exp2(s - m)
                l = lane_sum(p)
                acc = jnp.dot(p.astype(BF16), v, preferred_element_type=F32)
            else:
                m_new = jnp.maximum(m, ms)
                alpha = jnp.exp2(m - m_new)
                p = jnp.exp2(s - m_new)
                l = alpha * l + lane_sum(p)
                acc = alpha * acc + jnp.dot(p.astype(BF16), v, preferred_element_type=F32)
                m = m_new
        return acc / l.sum(axis=-1, keepdims=True)

    if diff:
        lam_ref, g_ref = refs[1 + 2 * n_pieces], refs[2 + 2 * n_pieces]
        lv = lam_ref[...]
        lam = (jnp.exp(jnp.sum(lv[0:1] * lv[1:2], axis=-1, keepdims=True))
               - jnp.exp(jnp.sum(lv[2:3] * lv[3:4], axis=-1, keepdims=True)) + lam_init)
        for h in range(hp):
            o = one_head(h)
            oc = o[:tq] - lam * o[tq:]
            o_ref[0, :, h * LANE:(h + 1) * LANE] = (_rms(oc, g_ref[...]) * (1.0 - lam_init)).astype(BF16)
    else:
        for h in range(0, hp, 2):
            o_ref[0, :, (h // 2) * LANE:(h // 2 + 1) * LANE] = (one_head(h) + one_head(h + 1)).astype(BF16)


KEY_CHUNK = 256


def _attn_call(q, pieces, n_heads, diff, tq, hp, extra=(), lam_init=0.0):
    b, seq = q.shape[0], q.shape[1]
    ow = hp * LANE if diff else hp * V_A
    in_specs = [pl.BlockSpec((1, tq, hp * LANE), lambda bi, h, qi: (bi, qi, h))]
    args = [q]
    chunks = []
    for pi, (k, kc, v, vc) in enumerate(pieces):
        s = k.shape[1]
        in_specs += [pl.BlockSpec((1, s, hp * LANE), lambda bi, h, qi, kc=kc: (bi, 0, kc // hp + h)),
                     pl.BlockSpec((1, s, hp * LANE), lambda bi, h, qi, vc=vc: (bi, 0, vc // hp + h))]
        args += [k, v]
        step = _tile(s, KEY_CHUNK)
        chunks += [(pi, st, step) for st in range(0, s, step)]
    for e in extra:
        in_specs.append(pl.BlockSpec(e.shape, lambda bi, h, qi: (0, 0)))
        args.append(e)
    return pl.pallas_call(
        functools.partial(_attn_kernel, chunks=tuple(chunks), n_pieces=len(pieces), diff=diff, tq=tq, hp=hp,
                          lam_init=lam_init),
        grid=(b, n_heads // hp, seq // tq),
        in_specs=in_specs,
        out_specs=pl.BlockSpec((1, tq, ow), lambda bi, h, qi: (bi, qi, h)),
        out_shape=jax.ShapeDtypeStruct((b, seq, (n_heads // hp) * ow), BF16),
        compiler_params=_cparams("parallel", "parallel", "arbitrary"),
        name="diff_attn" if diff else "mla_attn",
    )(*args)


def _conv3(x, prev_ref, next_ref, w_ref, b_ref, tm, seq):
    i = pl.program_id(0)
    first = (i * tm) % seq == 0
    last = ((i + 1) * tm) % seq == 0
    pv = prev_ref[...].astype(F32)[prev_ref.shape[0] - 1:, :] * jnp.where(first, 0.0, 1.0)
    nx = next_ref[...].astype(F32)[0:1, :] * jnp.where(last, 0.0, 1.0)
    row = lax.broadcasted_iota(jnp.int32, x.shape, 0)
    xp = jnp.where(row == 0, pv, pltpu.roll(x, 1, 0))
    xn = jnp.where(row == tm - 1, nx, pltpu.roll(x, tm - 1, 0))
    return w_ref[0:1, :] * xp + w_ref[1:2, :] * x + w_ref[2:3, :] * xn + b_ref[...]


HALO = 16


def _halo_specs(tm, width, col, t):
    r = tm // HALO
    nblk = t // HALO
    return [
        pl.BlockSpec((tm, width), lambda i: (i, col)),
        pl.BlockSpec((HALO, width), lambda i: (jnp.maximum(i * r - 1, 0), col)),
        pl.BlockSpec((HALO, width), lambda i: (jnp.minimum((i + 1) * r, nblk - 1), col)),
    ]


def _ssd_conv_kernel(x_ref, prev_ref, next_ref, w_ref, b_ref, o_ref, *, tm, seq):
    o_ref[...] = _silu(_conv3(x_ref[...].astype(F32), prev_ref, next_ref, w_ref, b_ref, tm, seq))


def _ssd_conv_call(proj, w, b, seq, tm):
    t = proj.shape[0]
    return pl.pallas_call(
        functools.partial(_ssd_conv_kernel, tm=tm, seq=seq),
        grid=(t // tm,),
        in_specs=_halo_specs(tm, CONV_CH, C_XBC // CONV_CH, t) + [
            pl.BlockSpec((3, CONV_CH), lambda i: (0, 0)),
            pl.BlockSpec((1, CONV_CH), lambda i: (0, 0)),
        ],
        out_specs=pl.BlockSpec((tm, CONV_CH), lambda i: (i, 0)),
        out_shape=jax.ShapeDtypeStruct((t, CONV_CH), F32),
        compiler_params=_cparams("parallel"),
        name="ssd_conv",
    )(proj, proj, proj, w, b)


def _cumsum_rows(x, reverse):
    n = x.shape[0]
    row = lax.broadcasted_iota(jnp.int32, x.shape, 0)
    s = 1
    while s < n:
        if reverse:
            x = x + jnp.where(row < n - s, pltpu.roll(x, n - s, 0), 0.0)
        else:
            x = x + jnp.where(row >= s, pltpu.roll(x, s, 0), 0.0)
        s *= 2
    return x


def _ssd_kernel(*refs, nc, has_h0):
    if has_h0:
        xs_ref, bs_ref, cs_ref, dt_ref, dtb_ref, alog_ref, dsk_ref, h0_ref = refs[:8]
        y_ref, hfin_ref, h_scr, hb_scr = refs[8:]
    else:
        xs_ref, bs_ref, cs_ref, dt_ref, dtb_ref, alog_ref, dsk_ref = refs[:7]
        y_ref, hfin_ref, h_scr, hb_scr = refs[7:]
    ph = pl.program_id(1)
    k = pl.program_id(2)
    q = CHUNK
    hpg = H_B // G_B
    gw = hpg * P_B
    dtp = jax.nn.softplus(dt_ref[0] + dtb_ref[...])
    la = dtp * (-jnp.exp(alog_ref[...]))
    lane = lax.broadcasted_iota(jnp.int32, (q, LANE), 1)
    row = lax.broadcasted_iota(jnp.int32, (q, LANE), 0)
    nt_dims = (((1,), (1,)), ((), ()))

    def init_state(d):
        if has_h0:
            h_scr[...] = h0_ref[0, d]
        else:
            h_scr[...] = jnp.zeros_like(h_scr)

    def update_state(w, tot, off):
        w_t = w.T
        for g in range(G_B):
            bg = bs_ref[0, :, g * N_STATE:(g + 1) * N_STATE].astype(BF16)
            for pr in range(hpg // 2):
                hi = g * hpg + 2 * pr
                r0 = hi * P_B
                x_t = xs_ref[0, :, (hi // 2) * LANE:(hi // 2 + 1) * LANE].T
                w_rows = jnp.where(row < P_B, w_t[off + hi:off + hi + 1, :], w_t[off + hi + 1:off + hi + 2, :])
                st = jnp.dot((x_t * w_rows).astype(BF16), bg, preferred_element_type=F32)
                dec = jnp.where(row < P_B, jnp.exp(tot[:, off + hi:off + hi + 1]),
                                jnp.exp(tot[:, off + hi + 1:off + hi + 2]))
                h_scr[r0:r0 + 2 * P_B, :] = h_scr[r0:r0 + 2 * P_B, :] * dec + st

    @pl.when(ph == 0)
    def _():
        c = nc - 1 - k

        @pl.when(k == 0)
        def _():
            init_state(1)

        suf = _cumsum_rows(la, True)
        hb_scr[c] = h_scr[...]
        tot = suf[0:1, :]
        update_state(jnp.exp(tot - suf) * dtp, tot, H_B)

        @pl.when(k == nc - 1)
        def _():
            hfin_ref[0, 1] = h_scr[...]

    @pl.when(ph == 1)
    def _():
        @pl.when(k == 0)
        def _():
            init_state(0)

        pre = _cumsum_rows(la, False)
        suf = _cumsum_rows(la, True)
        pre_t = pre.T
        suf_t = suf.T
        dt_t = dtp.T
        lower = row >= lane
        upper = row <= lane
        hb = hb_scr[k]
        for g in range(G_B):
            cg = cs_ref[0, :, g * N_STATE:(g + 1) * N_STATE].astype(BF16)
            bg = bs_ref[0, :, g * N_STATE:(g + 1) * N_STATE].astype(BF16)
            cb = lax.dot_general(cg, bg, nt_dims, preferred_element_type=F32)
            hf_g = h_scr[g * gw:(g + 1) * gw, :].astype(BF16)
            hb_g = hb[g * gw:(g + 1) * gw, :].astype(BF16)
            y_f = lax.dot_general(cg, hf_g, nt_dims, preferred_element_type=F32)
            y_b = lax.dot_general(cg, hb_g, nt_dims, preferred_element_type=F32)
            for pr in range(hpg // 2):
                hi0 = g * hpg + 2 * pr
                cols = slice((hi0 // 2) * LANE, (hi0 // 2 + 1) * LANE)
                xp = xs_ref[0, :, cols]
                xpb = xp.astype(BF16)
                acc = None
                e_f, e_b = [], []
                for e in range(2):
                    hi = hi0 + e
                    hb_i = H_B + hi
                    col_p = jnp.broadcast_to(pre[:, hi:hi + 1], (q, LANE))
                    col_s = jnp.broadcast_to(suf[:, hb_i:hb_i + 1], (q, LANE))
                    lf = jnp.exp(jnp.where(lower, col_p - pre_t[hi:hi + 1, :], -jnp.inf))
                    lb = jnp.exp(jnp.where(upper, col_s - suf_t[hb_i:hb_i + 1, :], -jnp.inf))
                    mm = cb * (lf * dt_t[hi:hi + 1, :] + lb * dt_t[hb_i:hb_i + 1, :])
                    keep = (lane < P_B) if e == 0 else (lane >= P_B)
                    xm = jnp.where(keep, xpb, jnp.zeros_like(xpb))
                    d = jnp.dot(mm.astype(BF16), xm, preferred_element_type=F32)
                    acc = d if acc is None else acc + d
                    e_f.append(jnp.exp(col_p))
                    e_b.append(jnp.exp(col_s))
                lo = pr * LANE
                y = (acc
                     + y_f[:, lo:lo + LANE] * jnp.where(lane < P_B, e_f[0], e_f[1])
                     + y_b[:, lo:lo + LANE] * jnp.where(lane < P_B, e_b[0], e_b[1])
                     + dsk_ref[:, cols] * xp)
                y_ref[0, :, cols] = y
        tot = pre[q - 1:q, :]
        update_state(jnp.exp(tot - pre) * dtp, tot, 0)

        @pl.when(k == nc - 1)
        def _():
            hfin_ref[0, 0] = h_scr[...]


def _ssd_call(xbc, dt_src, dt_col, dtb, alog, dsk, h0):
    b, seq = xbc.shape[0], xbc.shape[1]
    nc = seq // CHUNK
    has_h0 = h0 is not None

    def cidx(ph, k):
        return jnp.where(ph == 0, nc - 1 - k, k)

    in_specs = [
        pl.BlockSpec((1, CHUNK, D_INNER), lambda bi, ph, k: (bi, cidx(ph, k), 0)),
        pl.BlockSpec((1, CHUNK, G_B * N_STATE), lambda bi, ph, k: (bi, cidx(ph, k), D_INNER // (G_B * N_STATE))),
        pl.BlockSpec((1, CHUNK, G_B * N_STATE), lambda bi, ph, k: (bi, cidx(ph, k), D_INNER // (G_B * N_STATE) + 1)),
        pl.BlockSpec((1, CHUNK, LANE), lambda bi, ph, k: (bi, cidx(ph, k), dt_col)),
        pl.BlockSpec((1, LANE), lambda bi, ph, k: (0, 0)),
        pl.BlockSpec((1, LANE), lambda bi, ph, k: (0, 0)),
        pl.BlockSpec((1, D_INNER), lambda bi, ph, k: (0, 0)),
    ]
    args = [xbc, xbc, xbc, dt_src, dtb, alog, dsk]
    if has_h0:
        in_specs.append(pl.BlockSpec((1, 2, D_INNER, N_STATE), lambda bi, ph, k: (bi, 0, 0, 0)))
        args.append(h0)
    return pl.pallas_call(
        functools.partial(_ssd_kernel, nc=nc, has_h0=has_h0),
        grid=(b, 2, nc),
        in_specs=in_specs,
        out_specs=[
            pl.BlockSpec((1, CHUNK, D_INNER), lambda bi, ph, k: (bi, ph * k, 0)),
            pl.BlockSpec((1, 2, D_INNER, N_STATE), lambda bi, ph, k: (bi, 0, 0, 0)),
        ],
        out_shape=[
            jax.ShapeDtypeStruct((b, seq, D_INNER), F32),
            jax.ShapeDtypeStruct((b, 2, D_INNER, N_STATE), F32),
        ],
        scratch_shapes=[
            pltpu.VMEM((D_INNER, N_STATE), F32),
            pltpu.VMEM((nc, D_INNER, N_STATE), F32),
        ],
        compiler_params=_cparams("parallel", "arbitrary", "arbitrary"),
        name="ssd",
    )(*args)


def _post_kernel(oa_ref, y_ref, z_ref, oc_ref, gate_ref, x_ref, mod_ref, gb_ref, g2_ref,
                 wa_ref, wb_ref, wc_ref, wo_ref, xo_ref, h2_ref):
    yb = _rms(y_ref[...] * _silu(z_ref[...].astype(F32)), gb_ref[...]).astype(BF16)
    out_a = jnp.dot(oa_ref[...], wa_ref[...], preferred_element_type=F32)
    out_b = jnp.dot(yb, wb_ref[...], preferred_element_type=F32)
    out_c = jnp.dot(oc_ref[...], wc_ref[...], preferred_element_type=F32)
    merged = (jax.nn.sigmoid(gate_ref[:, 0:D_MODEL].astype(F32)) * out_a
              + jax.nn.sigmoid(gate_ref[:, D_MODEL:2 * D_MODEL].astype(F32)) * out_b
              + jax.nn.sigmoid(gate_ref[:, 2 * D_MODEL:3 * D_MODEL].astype(F32)) * out_c)
    out = jnp.dot(merged.astype(BF16), wo_ref[...], preferred_element_type=F32)
    xn = x_ref[...] + mod_ref[0, 2:3, :] * out
    xo_ref[...] = xn
    h2 = _rms(xn, g2_ref[...]) * (1.0 + mod_ref[0, 4:5, :]) + mod_ref[0, 3:4, :]
    h2_ref[...] = h2.astype(BF16)


def _post_call(oa, y, proj, oc, x2d, mod, gb, g2, wa, wb, wc, wo, row_fn, tm):
    t = x2d.shape[0]
    full = lambda i: (0, 0)
    return pl.pallas_call(
        _post_kernel,
        grid=(t // tm,),
        in_specs=[
            pl.BlockSpec((tm, H_A * V_A), lambda i: (i, 0)),
            pl.BlockSpec((tm, D_INNER), lambda i: (i, 0)),
            pl.BlockSpec((tm, D_INNER), lambda i: (i, C_Z // D_INNER)),
            pl.BlockSpec((tm, H_C * DV_C), lambda i: (i, 0)),
            pl.BlockSpec((tm, 3 * D_MODEL), lambda i: (i, 0)),
            pl.BlockSpec((tm, D_MODEL), lambda i: (i, 0)),
            pl.BlockSpec((1, N_MOD, D_MODEL), lambda i: (row_fn(i, tm), 0, 0)),
            pl.BlockSpec((1, D_INNER), full),
            pl.BlockSpec((1, D_MODEL), full),
            pl.BlockSpec((H_A * V_A, D_MODEL), full),
            pl.BlockSpec((D_INNER, D_MODEL), full),
            pl.BlockSpec((H_C * DV_C, D_MODEL), full),
            pl.BlockSpec((D_MODEL, D_MODEL), full),
        ],
        out_specs=[
            pl.BlockSpec((tm, D_MODEL), lambda i: (i, 0)),
            pl.BlockSpec((tm, D_MODEL), lambda i: (i, 0)),
        ],
        out_shape=[
            jax.ShapeDtypeStruct((t, D_MODEL), F32),
            jax.ShapeDtypeStruct((t, D_MODEL), BF16),
        ],
        compiler_params=_cparams("parallel"),
        name="post_mix",
    )(oa, y, proj, oc, proj, x2d, mod, gb, g2, wa, wb, wc, wo)


FFN_CHUNK = 256


def _ffn_kernel(h_ref, prev_ref, next_ref, wu_ref, cw_ref, cb_ref, wd_ref, x_ref, mod_ref, o_ref, a_scr, *, tm, seq):
    i = pl.program_id(0)
    hext = jnp.concatenate([prev_ref[...], h_ref[...], next_ref[...]], axis=0)
    pos = (i * tm + lax.broadcasted_iota(jnp.int32, (tm, 1), 0)) % seq
    has_prev = pos != 0
    has_next = pos != seq - 1
    ext = tm + 2 * HALO

    def conv(u, c0):
        up = pltpu.roll(u, 1, 0)[HALO:HALO + tm]
        un = pltpu.roll(u, ext - 1, 0)[HALO:HALO + tm]
        cols = slice(c0, c0 + FFN_CHUNK)
        return (cw_ref[0:1, cols] * jnp.where(has_prev, up, 0.0) + cw_ref[1:2, cols] * u[HALO:HALO + tm]
                + cw_ref[2:3, cols] * jnp.where(has_next, un, 0.0) + cb_ref[:, cols])

    for c in range(D_FF // FFN_CHUNK):
        g0, v0 = c * FFN_CHUNK, D_FF + c * FFN_CHUNK
        ug = jnp.dot(hext, wu_ref[:, g0:g0 + FFN_CHUNK], preferred_element_type=F32)
        uv = jnp.dot(hext, wu_ref[:, v0:v0 + FFN_CHUNK], preferred_element_type=F32)
        a_scr[:, g0:g0 + FFN_CHUNK] = (_silu(conv(ug, g0)) * conv(uv, v0)).astype(BF16)
    out = jnp.dot(a_scr[...], wd_ref[...], preferred_element_type=F32)
    o_ref[...] = x_ref[...] + mod_ref[0, 5:6, :] * out


def _ffn_call(h2, wu, cw, cb, wd, x2d, mod, row_fn, seq, tm):
    t = h2.shape[0]
    n = 2 * D_FF
    full = lambda i: (0, 0)
    return pl.pallas_call(
        functools.partial(_ffn_kernel, tm=tm, seq=seq),
        grid=(t // tm,),
        in_specs=_halo_specs(tm, D_MODEL, 0, t) + [
            pl.BlockSpec((D_MODEL, n), full),
            pl.BlockSpec((3, n), full),
            pl.BlockSpec((1, n), full),
            pl.BlockSpec((D_FF, D_MODEL), full),
            pl.BlockSpec((tm, D_MODEL), lambda i: (i, 0)),
            pl.BlockSpec((1, N_MOD, D_MODEL), lambda i: (row_fn(i, tm), 0, 0)),
        ],
        out_specs=pl.BlockSpec((tm, D_MODEL), lambda i: (i, 0)),
        out_shape=jax.ShapeDtypeStruct((t, D_MODEL), F32),
        scratch_shapes=[pltpu.VMEM((tm, D_FF), BF16)],
        compiler_params=_cparams("parallel"),
        name="ffn",
    )(h2, h2, h2, wu, cw, cb, wd, x2d, mod)


def _final_norm_kernel(x_ref, g_ref, o_ref):
    o_ref[...] = _rms(x_ref[...], g_ref[...])


def _final_norm_call(x2d, g, tm):
    t = x2d.shape[0]
    return pl.pallas_call(
        _final_norm_kernel,
        grid=(t // tm,),
        in_specs=[pl.BlockSpec((tm, D_MODEL), lambda i: (i, 0)), pl.BlockSpec((1, D_MODEL), lambda i: (0, 0))],
        out_specs=pl.BlockSpec((tm, D_MODEL), lambda i: (i, 0)),
        out_shape=jax.ShapeDtypeStruct((t, D_MODEL), F32),
        compiler_params=_cparams("parallel"),
        name="final_norm",
    )(x2d, g)


def _rope_tables(n_tokens, rot_dim, lead, tail):
    rows = n_tokens // GRID_W
    row = jnp.broadcast_to(jnp.arange(rows, dtype=F32)[:, None], (rows, GRID_W)).reshape(-1)
    col = jnp.broadcast_to(jnp.arange(GRID_W, dtype=F32)[None, :], (rows, GRID_W)).reshape(-1)
    n_freq = rot_dim // 4
    freqs = ROPE_THETA ** (-jnp.arange(n_freq, dtype=F32) / n_freq)
    ang = jnp.concatenate([row[:, None] * freqs, col[:, None] * freqs], axis=-1)
    cos, sin = jnp.cos(ang), jnp.sin(ang)
    c = jnp.repeat(cos, 2, axis=-1)
    s = jnp.stack([-sin, sin], axis=-1).reshape(n_tokens, rot_dim)
    reps = (LANE - lead - tail) // rot_dim
    c = jnp.tile(c, (1, reps))
    s = jnp.tile(s, (1, reps))
    c = jnp.concatenate([jnp.ones((n_tokens, lead), F32), c, jnp.ones((n_tokens, tail), F32)], axis=-1)
    s = jnp.concatenate([jnp.zeros((n_tokens, lead), F32), s, jnp.zeros((n_tokens, tail), F32)], axis=-1)
    return c, s


def _pad_w_in(w):
    cq, ckv, krope, z, xbc, dt, qc, kc, vc, gate = jnp.split(
        w, [384, 640, 672, 1696, 3744, 3776, 4288, 4800, 5312], axis=-1)
    d = w.shape[0]
    zeros = lambda n: jnp.zeros((d, n), w.dtype)
    kr_pad = jnp.concatenate([zeros(NOPE_A), krope, zeros(LANE - NOPE_A - ROPE_A)], axis=-1)
    dt_pad = jnp.concatenate([dt, zeros(LANE - 2 * H_B)], axis=-1)
    main = jnp.concatenate([gate, z, xbc, qc, kc, vc, cq, kr_pad, ckv], axis=-1)
    side_ctx = jnp.concatenate([kc, vc, ckv, kr_pad, dt_pad], axis=-1)
    return main.astype(BF16), side_ctx.astype(BF16), dt_pad.astype(BF16)


S_KC, S_VC, S_CKV, S_KROPE, S_DT = 0, 512, 1024, 1280, 1408


def _prep_layer(l, w):
    d_q = NOPE_A + ROPE_A
    w_uq = w['w_uq'][l].reshape(Q_RANK, H_A, d_q)
    w_uq = jnp.pad(w_uq, ((0, 0), (0, 0), (0, LANE - d_q))).reshape(Q_RANK, H_A * LANE)
    w_ukv = w['w_ukv'][l].reshape(KV_RANK, H_A, NOPE_A + V_A)
    wk = jnp.pad(w_ukv[:, :, :NOPE_A], ((0, 0), (0, 0), (0, LANE - NOPE_A))).reshape(KV_RANK, H_A * LANE)
    wv = w_ukv[:, :, NOPE_A:].reshape(KV_RANK, H_A // 2, 2, V_A)
    zv = jnp.zeros_like(wv[:, :, 0])
    wv = jnp.stack([jnp.concatenate([wv[:, :, 0], zv], -1), jnp.concatenate([zv, wv[:, :, 1]], -1)],
                   axis=2).reshape(KV_RANK, H_A * LANE)
    pad_lane = lambda v: jnp.pad(v.reshape(1, -1), ((0, 0), (0, LANE - v.size)))
    w_in, w_side_ctx, w_side_lat = _pad_w_in(w['w_in'][l])
    return dict(
        w_in=w_in, w_side_ctx=w_side_ctx, w_side_lat=w_side_lat,
        norm1_g=w['norm1_g'][l].reshape(1, -1),
        norm2_g=w['norm2_g'][l].reshape(1, -1),
        q_norm_g=w['q_norm_g'][l].reshape(1, -1),
        kv_norm_g=w['kv_norm_g'][l].reshape(1, -1),
        w_uq=w_uq.astype(BF16),
        w_ukv=jnp.concatenate([wk, wv], axis=-1).astype(BF16),
        w_oA=w['w_oA'][l].astype(BF16),
        w_oB=w['w_oB'][l].astype(BF16),
        w_oC=w['w_oC'][l].astype(BF16),
        w_out=w['w_out'][l].astype(BF16),
        ssd_conv_w=w['ssd_conv_w'][l],
        ssd_conv_b=w['ssd_conv_b'][l].reshape(1, -1),
        dt_bias=pad_lane(w['ssd_dt_bias'][l]),
        a_log=pad_lane(w['ssd_a_log'][l]),
        d_skip=jnp.repeat(w['ssd_d_skip'][l], P_B).reshape(1, -1),
        ssd_norm_g=w['ssd_norm_g'][l].reshape(1, -1),
        diff_lambda=w['diff_lambda'][l],
        diff_norm_g=w['diff_norm_g'][l].reshape(1, -1),
        ffn_w_up=w['ffn_w_up'][l].astype(BF16),
        ffn_conv_w=w['ffn_conv_w'][l],
        ffn_conv_b=w['ffn_conv_b'][l].reshape(1, -1),
        ffn_w_down=w['ffn_w_down'][l].astype(BF16),
    )


def _tile(n, pref):
    t = min(n, pref)
    while n % t:
        t //= 2
    return t


def _block(x, mod, lp, lam_init, row_fn, tabs, ctx):
    b, seq, _ = x.shape
    t = b * seq
    x2d = x.reshape(t, D_MODEL)
    is_ctx = ctx is None
    tm_big = _tile(t if is_ctx else seq, 1024)
    tm = _tile(seq, 256)
    proj, side = _inproj_call(x2d, lp['norm1_g'], mod, lp['w_in'],
                              lp['w_side_ctx'] if is_ctx else lp['w_side_lat'], row_fn, tm_big)
    proj3 = proj.reshape(b, seq, N_PROJ)
    tab_a, tab_c = tabs if tabs is not None else (None, None)
    qa = _mla_q_call(proj, lp['q_norm_g'], lp['w_uq'], tab_a, seq, tm_big)
    if is_ctx:
        k_new, v_new, ckv_n = _kv_up_call(side, S_CKV // KV_RANK, side, S_KROPE // LANE, lp['kv_norm_g'],
                                          lp['w_ukv'], None, seq, tm_big, want_ckv=True)
    else:
        k_new, v_new = _kv_up_call(proj, C_CKV // KV_RANK, proj, C_KROPE // LANE, lp['kv_norm_g'], lp['w_ukv'],
                                   tab_a, seq, tm_big)
    wide = H_A * LANE
    pieces_a = [(k_new.reshape(b, seq, wide), 0, v_new.reshape(b, seq, wide), 0)]
    qd, kd = _dprep_call(proj, tab_c, seq, tm_big)
    pieces_c = [(kd.reshape(b, seq, H_C * LANE), 0, proj3, C_VC // LANE)]
    h0 = None
    if not is_ctx:
        ctx_ckv, ctx_krope, ctx_k, ctx_v, h0 = ctx
        past = ctx_ckv.shape[1]
        kr_pad = jnp.pad(ctx_krope.reshape(b * past, ROPE_A), ((0, 0), (NOPE_A, LANE - NOPE_A - ROPE_A)))
        k_c, v_c = _kv_up_call(ctx_ckv.reshape(b * past, KV_RANK), 0, kr_pad, 0, None, lp['w_ukv'],
                               None, past, _tile(b * past, 1024))
        pieces_a = [(k_c.reshape(b, past, wide), 0, v_c.reshape(b, past, wide), 0)] + pieces_a
        pieces_c = [(ctx_k.reshape(b, past, H_C * LANE), 0, ctx_v.reshape(b, past, H_C * DV_C), 0)] + pieces_c
        h0 = h0.reshape(b, 2, D_INNER, N_STATE)
    oa = _attn_call(qa.reshape(b, seq, wide), pieces_a, H_A, False, _tile(seq, 1024), H_A if is_ctx else 2)
    oc = _attn_call(qd.reshape(b, seq, H_C * LANE), pieces_c, H_C, True, _tile(seq, 512), H_C if is_ctx else 1,
                    extra=(lp['diff_lambda'], lp['diff_norm_g']), lam_init=lam_init)
    xbc = _ssd_conv_call(proj, lp['ssd_conv_w'], lp['ssd_conv_b'], seq, tm)
    y, hfin = _ssd_call(xbc.reshape(b, seq, CONV_CH), side.reshape(b, seq, side.shape[1]),
                        S_DT // LANE if is_ctx else 0, lp['dt_bias'], lp['a_log'], lp['d_skip'], h0)
    x2d, h2 = _post_call(oa.reshape(t, H_A * V_A), y.reshape(t, D_INNER), proj, oc.reshape(t, H_C * DV_C), x2d, mod,
                         lp['ssd_norm_g'], lp['norm2_g'], lp['w_oA'], lp['w_oB'], lp['w_oC'], lp['w_out'],
                         row_fn, tm)
    x2d = _ffn_call(h2, lp['ffn_w_up'], lp['ffn_conv_w'], lp['ffn_conv_b'], lp['ffn_w_down'], x2d, mod, row_fn,
                    seq, _tile(t if is_ctx else seq, 1024))
    st = None
    if is_ctx:
        st = (ckv_n.reshape(b, seq, KV_RANK),
              side[:, S_KROPE + NOPE_A:S_KROPE + NOPE_A + ROPE_A].reshape(b, seq, ROPE_A),
              side[:, S_KC:S_KC + H_C * 2 * DH_C].reshape(b, seq, H_C, 2, DH_C),
              side[:, S_VC:S_VC + H_C * DV_C].reshape(b, seq, H_C, DV_C),
              hfin.reshape(b, 2, H_B, P_B, N_STATE))
    return x2d.reshape(b, seq, D_MODEL), st


def kernel(x_prompt, x_sample, cache_mla_ckv, cache_mla_krope, cache_diff_k, cache_diff_v, state_ssd, c, c_ctx, w_mod, b_mod, norm1_g, norm2_g, w_in, q_norm_g, kv_norm_g, w_uq, w_ukv, w_oA, ssd_conv_w, ssd_conv_b, ssd_dt_bias, ssd_a_log, ssd_d_skip, ssd_norm_g, w_oB, diff_lambda, diff_norm_g, w_oC, w_out, ffn_w_up, ffn_conv_w, ffn_conv_b, ffn_w_down, final_norm_g):
    weights = dict(norm1_g=norm1_g, norm2_g=norm2_g, w_in=w_in, q_norm_g=q_norm_g, kv_norm_g=kv_norm_g,
                   w_uq=w_uq, w_ukv=w_ukv, w_oA=w_oA, ssd_conv_w=ssd_conv_w, ssd_conv_b=ssd_conv_b,
                   ssd_dt_bias=ssd_dt_bias, ssd_a_log=ssd_a_log, ssd_d_skip=ssd_d_skip, ssd_norm_g=ssd_norm_g,
                   w_oB=w_oB, diff_lambda=diff_lambda, diff_norm_g=diff_norm_g, w_oC=w_oC, w_out=w_out,
                   ffn_w_up=ffn_w_up, ffn_conv_w=ffn_conv_w, ffn_conv_b=ffn_conv_b, ffn_w_down=ffn_w_down)
    n_dec, n_lat = x_sample.shape[0], x_sample.shape[1]
    ctx_row = n_dec
    cond = jnp.concatenate([c, c_ctx[None, :], jnp.zeros((COND_ROWS - n_dec - 1, D_MODEL), F32)], axis=0)
    mod_all = _mod_call(cond, w_mod, b_mod).reshape(DEPTH, COND_ROWS, N_MOD, D_MODEL)
    tabs = (_rope_tables(n_lat, ROPE_A, NOPE_A, LANE - NOPE_A - ROPE_A), _rope_tables(n_lat, DH_C, 0, 0))
    lat_row = lambda i, tm: (i * tm) // n_lat
    ctx_row_fn = lambda i, tm: ctx_row
    xp, xs = x_prompt, x_sample
    new = [[] for _ in range(5)]
    for l in range(DEPTH):
        lp = _prep_layer(l, weights)
        lam_init = 0.8 - 0.6 * math.exp(-0.3 * l)
        xp, st = _block(xp, mod_all[l], lp, lam_init, ctx_row_fn, None, None)
        for acc, v in zip(new, st):
            acc.append(v)
        ctx = (cache_mla_ckv[:, l], cache_mla_krope[:, l], cache_diff_k[:, l], cache_diff_v[:, l], state_ssd[:, l])
        xs, _ = _block(xs, mod_all[l], lp, lam_init, lat_row, tabs, ctx)
    bp, sp = xp.shape[0], xp.shape[1]
    y_prompt = _final_norm_call(xp.reshape(bp * sp, D_MODEL), final_norm_g.reshape(1, -1), _tile(bp * sp, 1024))
    y_sample = _final_norm_call(xs.reshape(n_dec * n_lat, D_MODEL), final_norm_g.reshape(1, -1),
                                _tile(n_dec * n_lat, 1024))
    return (y_prompt.reshape(bp, sp, D_MODEL), y_sample.reshape(n_dec, n_lat, D_MODEL),
            jnp.stack(new[0], axis=1), jnp.stack(new[1], axis=1), jnp.stack(new[2], axis=1),
            jnp.stack(new[3], axis=1), jnp.stack(new[4], axis=1))
```

```python
import functools
import math

import jax
import jax.numpy as jnp
from jax import lax
from jax.experimental import pallas as pl
from jax.experimental.pallas import tpu as pltpu

F32 = jnp.float32
BF16 = jnp.bfloat16

D_MODEL = 1024
DEPTH = 4
GRID_W = 64
ROPE_THETA = 10000.0
EPS = 1e-6
H_A, NOPE_A, ROPE_A, V_A = 8, 64, 32, 64
Q_RANK, KV_RANK = 384, 256
H_B, P_B, N_STATE, G_B, CHUNK = 16, 64, 128, 4, 128
D_INNER = H_B * P_B
CONV_CH = D_INNER + 2 * G_B * N_STATE
H_C, DH_C = 4, 64
DV_C = 2 * DH_C
D_FF = 2816
N_MOD = 6
COND_ROWS = 16

LOG2E = math.log2(math.e)
LANE = 128
SUBLANE = 8
VMEM_LIMIT = 56 * 1024 * 1024

C_GATE = 0
C_Z = 3072
C_XBC = 4096
C_QC = 6144
C_KC = 6656
C_VC = 7168
C_CQ = 7680
C_KROPE = 8064
C_CKV = 8192
N_PROJ = 8448
PROJ_TILE = 2816


def _cparams(*sem):
    return pltpu.CompilerParams(dimension_semantics=sem, vmem_limit_bytes=VMEM_LIMIT)


def _silu(x):
    return x * jax.nn.sigmoid(x)


def _rms(x, g):
    ms = jnp.mean(x * x, axis=-1, keepdims=True)
    return x * lax.rsqrt(ms + EPS) * g


def _rope(x, c, s):
    n = x.shape[-1]
    lane = lax.broadcasted_iota(jnp.int32, x.shape, x.ndim - 1)
    nxt = pltpu.roll(x, n - 1, x.ndim - 1)
    prv = pltpu.roll(x, 1, x.ndim - 1)
    return x * c + jnp.where(lane % 2 == 0, nxt, prv) * s


def _mod_kernel(c_ref, w_ref, b_ref, o_ref):
    s = _silu(c_ref[...]).astype(BF16)
    o_ref[0] = jnp.dot(s, w_ref[0].astype(BF16), preferred_element_type=F32) + b_ref[0]


def _mod_call(cond, w_mod, b_mod):
    tn = 1536
    n = N_MOD * D_MODEL
    return pl.pallas_call(
        _mod_kernel,
        grid=(DEPTH, n // tn),
        in_specs=[
            pl.BlockSpec((COND_ROWS, D_MODEL), lambda l, j: (0, 0)),
            pl.BlockSpec((1, D_MODEL, tn), lambda l, j: (l, 0, j)),
            pl.BlockSpec((1, 1, tn), lambda l, j: (l, 0, j)),
        ],
        out_specs=pl.BlockSpec((1, COND_ROWS, tn), lambda l, j: (l, 0, j)),
        out_shape=jax.ShapeDtypeStruct((DEPTH, COND_ROWS, n), F32),
        compiler_params=_cparams("parallel", "parallel"),
        name="mod",
    )(cond, w_mod, b_mod.reshape(DEPTH, 1, n))


INPROJ_SUB = 4


def _inproj_kernel(x_ref, g_ref, mod_ref, w_ref, ws_ref, o_ref, os_ref, h_scr):
    j = pl.program_id(1)

    @pl.when(j == 0)
    def _():
        sub = x_ref.shape[0] // INPROJ_SUB
        for r in range(INPROJ_SUB):
            rows = slice(r * sub, (r + 1) * sub)
            h = _rms(x_ref[rows, :], g_ref[...]) * (1.0 + mod_ref[0, 1:2, :]) + mod_ref[0, 0:1, :]
            hb = h.astype(BF16)
            h_scr[rows, :] = hb
            os_ref[rows, :] = jnp.dot(hb, ws_ref[...], preferred_element_type=F32)
            o_ref[rows, :] = jnp.dot(hb, w_ref[...], preferred_element_type=F32).astype(BF16)

    @pl.when(j > 0)
    def _():
        o_ref[...] = jnp.dot(h_scr[...], w_ref[...], preferred_element_type=F32).astype(BF16)


def _inproj_call(x2d, g, mod, w, w_side, row_fn, tm):
    t = x2d.shape[0]
    tn = PROJ_TILE
    n_side = w_side.shape[1]
    return pl.pallas_call(
        _inproj_kernel,
        grid=(t // tm, N_PROJ // tn),
        in_specs=[
            pl.BlockSpec((tm, D_MODEL), lambda i, j: (i, 0)),
            pl.BlockSpec((1, D_MODEL), lambda i, j: (0, 0)),
            pl.BlockSpec((1, N_MOD, D_MODEL), lambda i, j: (row_fn(i, tm), 0, 0)),
            pl.BlockSpec((D_MODEL, tn), lambda i, j: (0, j)),
            pl.BlockSpec((D_MODEL, n_side), lambda i, j: (0, 0)),
        ],
        out_specs=[
            pl.BlockSpec((tm, tn), lambda i, j: (i, j)),
            pl.BlockSpec((tm, n_side), lambda i, j: (i, 0)),
        ],
        out_shape=[
            jax.ShapeDtypeStruct((t, N_PROJ), BF16),
            jax.ShapeDtypeStruct((t, n_side), F32),
        ],
        scratch_shapes=[pltpu.VMEM((tm, D_MODEL), BF16)],
        compiler_params=_cparams("parallel", "arbitrary"),
        name="inproj",
    )(x2d, g, mod, w, w_side)


def _mla_q_kernel(*refs, rope, scale):
    if rope:
        p_ref, g_ref, w_ref, c_ref, s_ref, o_ref = refs
    else:
        p_ref, g_ref, w_ref, o_ref = refs
    cqn = _rms(p_ref[:, :Q_RANK].astype(F32), g_ref[...]).astype(BF16)
    q = jnp.dot(cqn, w_ref[...], preferred_element_type=F32)
    for h in range(H_A):
        qh = q[:, h * LANE:(h + 1) * LANE]
        if rope:
            qh = _rope(qh, c_ref[...], s_ref[...])
        o_ref[:, h * LANE:(h + 1) * LANE] = (qh * scale).astype(BF16)


def _mla_q_call(proj, g, w, tabs, seq, tm):
    t = proj.shape[0]
    rope = tabs is not None
    nb = seq // tm
    in_specs = [
        pl.BlockSpec((tm, 512), lambda i: (i, C_CQ // 512)),
        pl.BlockSpec((1, Q_RANK), lambda i: (0, 0)),
        pl.BlockSpec((Q_RANK, H_A * LANE), lambda i: (0, 0)),
    ]
    args = [proj, g, w]
    if rope:
        in_specs += [pl.BlockSpec((tm, LANE), lambda i: (i % nb, 0))] * 2
        args += list(tabs)
    return pl.pallas_call(
        functools.partial(_mla_q_kernel, rope=rope, scale=LOG2E * (NOPE_A + ROPE_A) ** -0.5),
        grid=(t // tm,),
        in_specs=in_specs,
        out_specs=pl.BlockSpec((tm, H_A * LANE), lambda i: (i, 0)),
        out_shape=jax.ShapeDtypeStruct((t, H_A * LANE), BF16),
        compiler_params=_cparams("parallel"),
        name="mla_q",
    )(*args)


def _kv_up_kernel(*refs, norm, rope, want_ckv):
    refs = list(refs)
    ckv_ref, kr_ref = refs[0], refs[1]
    pos = 2
    if norm:
        g_ref = refs[pos]
        pos += 1
    w_ref = refs[pos]
    pos += 1
    if rope:
        c_ref, s_ref = refs[pos], refs[pos + 1]
        pos += 2
    k_ref, v_ref = refs[pos], refs[pos + 1]
    ckv = ckv_ref[...].astype(F32)
    if norm:
        ckv = _rms(ckv, g_ref[...])
        if want_ckv:
            refs[pos + 2][...] = ckv
    kv = jnp.dot(ckv.astype(BF16), w_ref[...], preferred_element_type=F32)
    kr = kr_ref[...].astype(F32)
    if rope:
        kr = _rope(kr, c_ref[...], s_ref[...])
    for h in range(H_A):
        k_ref[:, h * LANE:(h + 1) * LANE] = (kv[:, h * LANE:(h + 1) * LANE] + kr).astype(BF16)
    v_ref[...] = kv[:, H_A * LANE:].astype(BF16)


def _kv_up_call(ckv_src, ckv_col, kr_src, kr_col, g, w, tabs, seq, tm, want_ckv=False):
    t = ckv_src.shape[0]
    norm = g is not None
    rope = tabs is not None
    nb = seq // tm
    in_specs = [
        pl.BlockSpec((tm, KV_RANK), lambda i: (i, ckv_col)),
        pl.BlockSpec((tm, LANE), lambda i: (i, kr_col)),
    ]
    args = [ckv_src, kr_src]
    if norm:
        in_specs.append(pl.BlockSpec((1, KV_RANK), lambda i: (0, 0)))
        args.append(g)
    in_specs.append(pl.BlockSpec((KV_RANK, 2 * H_A * LANE), lambda i: (0, 0)))
    args.append(w)
    if rope:
        in_specs += [pl.BlockSpec((tm, LANE), lambda i: (i % nb, 0))] * 2
        args += list(tabs)
    out_specs = [pl.BlockSpec((tm, H_A * LANE), lambda i: (i, 0))] * 2
    out_shape = [jax.ShapeDtypeStruct((t, H_A * LANE), BF16)] * 2
    if want_ckv:
        out_specs.append(pl.BlockSpec((tm, KV_RANK), lambda i: (i, 0)))
        out_shape.append(jax.ShapeDtypeStruct((t, KV_RANK), F32))
    return pl.pallas_call(
        functools.partial(_kv_up_kernel, norm=norm, rope=rope, want_ckv=want_ckv),
        grid=(t // tm,),
        in_specs=in_specs,
        out_specs=out_specs,
        out_shape=out_shape,
        compiler_params=_cparams("parallel"),
        name="kv_up",
    )(*args)


def _dprep_kernel(*refs, rope):
    if rope:
        q_ref, k_ref, c_ref, s_ref, qo_ref, ko_ref = refs
    else:
        q_ref, k_ref, qo_ref, ko_ref = refs
    for b in range(H_C):
        q = q_ref[:, b * LANE:(b + 1) * LANE].astype(F32)
        if rope:
            q = _rope(q, c_ref[...], s_ref[...])
        qo_ref[:, b * LANE:(b + 1) * LANE] = (q * (LOG2E * DH_C ** -0.5)).astype(BF16)
    for b in range(H_C):
        k = k_ref[:, b * LANE:(b + 1) * LANE].astype(F32)
        if rope:
            k = _rope(k, c_ref[...], s_ref[...])
        ko_ref[:, b * LANE:(b + 1) * LANE] = k.astype(BF16)


def _dprep_call(proj, tabs, seq, tm):
    t = proj.shape[0]
    rope = tabs is not None
    nb = seq // tm
    in_specs = [
        pl.BlockSpec((tm, 512), lambda i: (i, C_QC // 512)),
        pl.BlockSpec((tm, 512), lambda i: (i, C_KC // 512)),
    ]
    args = [proj, proj]
    if rope:
        in_specs += [pl.BlockSpec((tm, LANE), lambda i: (i % nb, 0))] * 2
        args += list(tabs)
    return pl.pallas_call(
        functools.partial(_dprep_kernel, rope=rope),
        grid=(t // tm,),
        in_specs=in_specs,
        out_specs=[
            pl.BlockSpec((tm, 512), lambda i: (i, 0)),
            pl.BlockSpec((tm, 512), lambda i: (i, 0)),
        ],
        out_shape=[
            jax.ShapeDtypeStruct((t, 512), BF16),
            jax.ShapeDtypeStruct((t, 512), BF16),
        ],
        compiler_params=_cparams("parallel"),
        name="dprep",
    )(*args)


def _attn_kernel(*refs, chunks, n_pieces, diff, tq, hp, lam_init):
    q_ref = refs[0]
    kv = refs[1:1 + 2 * n_pieces]
    o_ref = refs[-1]
    nt = (((1,), (1,)), ((), ()))

    def lane_sum(p):
        parts = [p[:, c * LANE:(c + 1) * LANE] for c in range(p.shape[1] // LANE)]
        while len(parts) > 1:
            parts = [a + b for a, b in zip(parts[::2], parts[1::2])] + ([parts[-1]] if len(parts) % 2 else [])
        return parts[0]

    def one_head(h):
        cols = slice(h * LANE, (h + 1) * LANE)
        q = q_ref[0, :, cols]
        if diff:
            lane = lax.broadcasted_iota(jnp.int32, q.shape, 1)
            zero = jnp.zeros_like(q)
            q = jnp.concatenate([jnp.where(lane < DH_C, q, zero), jnp.where(lane >= DH_C, q, zero)], axis=0)
        m = l = acc = None
        for pi, st, sz in chunks:
            k = kv[2 * pi][0, st:st + sz, cols].astype(BF16)
            v = kv[2 * pi + 1][0, st:st + sz, cols].astype(BF16)
            s = lax.dot_general(q, k, nt, preferred_element_type=F32)
            ms = s.max(axis=-1, keepdims=True)
            if m is None:
                m = ms
                p = jnp.exp2(s - m)
                l = lane_sum(p)
                acc = jnp.dot(p.astype(BF16), v, preferred_element_type=F32)
            else:
                m_new = jnp.maximum(m, ms)
                alpha = jnp.exp2(m - m_new)
                p = jnp.exp2(s - m_new)
                l = alpha * l + lane_sum(p)
                acc = alpha * acc + jnp.dot(p.astype(BF16), v, preferred_element_type=F32)
                m = m_new
        return acc / l.sum(axis=-1, keepdims=True)

    if diff:
        lam_ref, g_ref = refs[1 + 2 * n_pieces], refs[2 + 2 * n_pieces]
        lv = lam_ref[...]
        lam = (jnp.exp(jnp.sum(lv[0:1] * lv[1:2], axis=-1, keepdims=True))
               - jnp.exp(jnp.sum(lv[2:3] * lv[3:4], axis=-1, keepdims=True)) + lam_init)
        for h in range(hp):
            o = one_head(h)
            oc = o[:tq] - lam * o[tq:]
            o_ref[0, :, h * LANE:(h + 1) * LANE] = (_rms(oc, g_ref[...]) * (1.0 - lam_init)).astype(BF16)
    else:
        for h in range(0, hp, 2):
            o_ref[0, :, (h // 2) * LANE:(h // 2 + 1) * LANE] = (one_head(h) + one_head(h + 1)).astype(BF16)


KEY_CHUNK = 256


def _attn_call(q, pieces, n_heads, diff, tq, hp, extra=(), lam_init=0.0):
    b, seq = q.shape[0], q.shape[1]
    ow = hp * LANE if diff else hp * V_A
    in_specs = [pl.BlockSpec((1, tq, hp * LANE), lambda bi, h, qi: (bi, qi, h))]
    args = [q]
    chunks = []
    for pi, (k, kc, v, vc) in enumerate(pieces):
        s = k.shape[1]
        in_specs += [pl.BlockSpec((1, s, hp * LANE), lambda bi, h, qi, kc=kc: (bi, 0, kc // hp + h)),
                     pl.BlockSpec((1, s, hp * LANE), lambda bi, h, qi, vc=vc: (bi, 0, vc // hp + h))]
        args += [k, v]
        step = _tile(s, KEY_CHUNK)
        chunks += [(pi, st, step) for st in range(0, s, step)]
    for e in extra:
        in_specs.append(pl.BlockSpec(e.shape, lambda bi, h, qi: (0, 0)))
        args.append(e)
    return pl.pallas_call(
        functools.partial(_attn_kernel, chunks=tuple(chunks), n_pieces=len(pieces), diff=diff, tq=tq, hp=hp,
                          lam_init=lam_init),
        grid=(b, n_heads // hp, seq // tq),
        in_specs=in_specs,
        out_specs=pl.BlockSpec((1, tq, ow), lambda bi, h, qi: (bi, qi, h)),
        out_shape=jax.ShapeDtypeStruct((b, seq, (n_heads // hp) * ow), BF16),
        compiler_params=_cparams("parallel", "parallel", "arbitrary"),
        name="diff_attn" if diff else "mla_attn",
    )(*args)


def _conv3(x, prev_ref, next_ref, w_ref, b_ref, tm, seq):
    i = pl.program_id(0)
    first = (i * tm) % seq == 0
    last = ((i + 1) * tm) % seq == 0
    pv = prev_ref[...].astype(F32)[prev_ref.shape[0] - 1:, :] * jnp.where(first, 0.0, 1.0)
    nx = next_ref[...].astype(F32)[0:1, :] * jnp.where(last, 0.0, 1.0)
    row = lax.broadcasted_iota(jnp.int32, x.shape, 0)
    xp = jnp.where(row == 0, pv, pltpu.roll(x, 1, 0))
    xn = jnp.where(row == tm - 1, nx, pltpu.roll(x, tm - 1, 0))
    return w_ref[0:1, :] * xp + w_ref[1:2, :] * x + w_ref[2:3, :] * xn + b_ref[...]


HALO = 16


def _halo_specs(tm, width, col, t):
    r = tm // HALO
    nblk = t // HALO
    return [
        pl.BlockSpec((tm, width), lambda i: (i, col)),
        pl.BlockSpec((HALO, width), lambda i: (jnp.maximum(i * r - 1, 0), col)),
        pl.BlockSpec((HALO, width), lambda i: (jnp.minimum((i + 1) * r, nblk - 1), col)),
    ]


def _ssd_conv_kernel(x_ref, prev_ref, next_ref, w_ref, b_ref, o_ref, *, tm, seq):
    o_ref[...] = _silu(_conv3(x_ref[...].astype(F32), prev_ref, next_ref, w_ref, b_ref, tm, seq))


def _ssd_conv_call(proj, w, b, seq, tm):
    t = proj.shape[0]
    return pl.pallas_call(
        functools.partial(_ssd_conv_kernel, tm=tm, seq=seq),
        grid=(t // tm,),
        in_specs=_halo_specs(tm, CONV_CH, C_XBC // CONV_CH, t) + [
            pl.BlockSpec((3, CONV_CH), lambda i: (0, 0)),
            pl.BlockSpec((1, CONV_CH), lambda i: (0, 0)),
        ],
        out_specs=pl.BlockSpec((tm, CONV_CH), lambda i: (i, 0)),
        out_shape=jax.ShapeDtypeStruct((t, CONV_CH), F32),
        compiler_params=_cparams("parallel"),
        name="ssd_conv",
    )(proj, proj, proj, w, b)


def _cumsum_rows(x, reverse):
    n = x.shape[0]
    row = lax.broadcasted_iota(jnp.int32, x.shape, 0)
    s = 1
    while s < n:
        if reverse:
            x = x + jnp.where(row < n - s, pltpu.roll(x, n - s, 0), 0.0)
        else:
            x = x + jnp.where(row >= s, pltpu.roll(x, s, 0), 0.0)
        s *= 2
    return x


def _ssd_kernel(*refs, nc, has_h0):
    if has_h0:
        xs_ref, bs_ref, cs_ref, dt_ref, dtb_ref, alog_ref, dsk_ref, h0_ref = refs[:8]
        y_ref, hfin_ref, h_scr, hb_scr = refs[8:]
    else:
        xs_ref, bs_ref, cs_ref, dt_ref, dtb_ref, alog_ref, dsk_ref = refs[:7]
        y_ref, hfin_ref, h_scr, hb_scr = refs[7:]
    ph = pl.program_id(1)
    k = pl.program_id(2)
    q = CHUNK
    hpg = H_B // G_B
    gw = hpg * P_B
    dtp = jax.nn.softplus(dt_ref[0] + dtb_ref[...])
    la = dtp * (-jnp.exp(alog_ref[...]))
    lane = lax.broadcasted_iota(jnp.int32, (q, LANE), 1)
    row = lax.broadcasted_iota(jnp.int32, (q, LANE), 0)
    nt_dims = (((1,), (1,)), ((), ()))

    def init_state(d):
        if has_h0:
            h_scr[...] = h0_ref[0, d]
        else:
            h_scr[...] = jnp.zeros_like(h_scr)

    def update_state(w, tot, off):
        w_t = w.T
        for g in range(G_B):
            bg = bs_ref[0, :, g * N_STATE:(g + 1) * N_STATE].astype(BF16)
            for pr in range(hpg // 2):
                hi = g * hpg + 2 * pr
                r0 = hi * P_B
                x_t = xs_ref[0, :, (hi // 2) * LANE:(hi // 2 + 1) * LANE].T
                w_rows = jnp.where(row < P_B, w_t[off + hi:off + hi + 1, :], w_t[off + hi + 1:off + hi + 2, :])
                st = jnp.dot((x_t * w_rows).astype(BF16), bg, preferred_element_type=F32)
                dec = jnp.where(row < P_B, jnp.exp(tot[:, off + hi:off + hi + 1]),
                                jnp.exp(tot[:, off + hi + 1:off + hi + 2]))
                h_scr[r0:r0 + 2 * P_B, :] = h_scr[r0:r0 + 2 * P_B, :] * dec + st

    @pl.when(ph == 0)
    def _():
        c = nc - 1 - k

        @pl.when(k == 0)
        def _():
            init_state(1)

        suf = _cumsum_rows(la, True)
        hb_scr[c] = h_scr[...]
        tot = suf[0:1, :]
        update_state(jnp.exp(tot - suf) * dtp, tot, H_B)

        @pl.when(k == nc - 1)
        def _():
            hfin_ref[0, 1] = h_scr[...]

    @pl.when(ph == 1)
    def _():
        @pl.when(k == 0)
        def _():
            init_state(0)

        pre = _cumsum_rows(la, False)
        suf = _cumsum_rows(la, True)
        pre_t = pre.T
        suf_t = suf.T
        dt_t = dtp.T
        lower = row >= lane
        upper = row <= lane
        hb = hb_scr[k]
        for g in range(G_B):
            cg = cs_ref[0, :, g * N_STATE:(g + 1) * N_STATE].astype(BF16)
            bg = bs_ref[0, :, g * N_STATE:(g + 1) * N_STATE].astype(BF16)
            cb = lax.dot_general(cg, bg, nt_dims, preferred_element_type=F32)
            hf_g = h_scr[g * gw:(g + 1) * gw, :].astype(BF16)
            hb_g = hb[g * gw:(g + 1) * gw, :].astype(BF16)
            y_f = lax.dot_general(cg, hf_g, nt_dims, preferred_element_type=F32)
            y_b = lax.dot_general(cg, hb_g, nt_dims, preferred_element_type=F32)
            for pr in range(hpg // 2):
                hi0 = g * hpg + 2 * pr
                cols = slice((hi0 // 2) * LANE, (hi0 // 2 + 1) * LANE)
                xp = xs_ref[0, :, cols]
                xpb = xp.astype(BF16)
                acc = None
                e_f, e_b = [], []
                for e in range(2):
                    hi = hi0 + e
                    hb_i = H_B + hi
                    col_p = jnp.broadcast_to(pre[:, hi:hi + 1], (q, LANE))
                    col_s = jnp.broadcast_to(suf[:, hb_i:hb_i + 1], (q, LANE))
                    lf = jnp.exp(jnp.where(lower, col_p - pre_t[hi:hi + 1, :], -jnp.inf))
                    lb = jnp.exp(jnp.where(upper, col_s - suf_t[hb_i:hb_i + 1, :], -jnp.inf))
                    mm = cb * (lf * dt_t[hi:hi + 1, :] + lb * dt_t[hb_i:hb_i + 1, :])
                    keep = (lane < P_B) if e == 0 else (lane >= P_B)
                    xm = jnp.where(keep, xpb, jnp.zeros_like(xpb))
                    d = jnp.dot(mm.astype(BF16), xm, preferred_element_type=F32)
                    acc = d if acc is None else acc + d
                    e_f.append(jnp.exp(col_p))
                    e_b.append(jnp.exp(col_s))
                lo = pr * LANE
                y = (acc
                     + y_f[:, lo:lo + LANE] * jnp.where(lane < P_B, e_f[0], e_f[1])
                     + y_b[:, lo:lo + LANE] * jnp.where(lane < P_B, e_b[0], e_b[1])
                     + dsk_ref[:, cols] * xp)
                y_ref[0, :, cols] = y
        tot = pre[q - 1:q, :]
        update_state(jnp.exp(tot - pre) * dtp, tot, 0)

        @pl.when(k == nc - 1)
        def _():
            hfin_ref[0, 0] = h_scr[...]


def _ssd_call(xbc, dt_src, dt_col, dtb, alog, dsk, h0):
    b, seq = xbc.shape[0], xbc.shape[1]
    nc = seq // CHUNK
    has_h0 = h0 is not None

    def cidx(ph, k):
        return jnp.where(ph == 0, nc - 1 - k, k)

    in_specs = [
        pl.BlockSpec((1, CHUNK, D_INNER), lambda bi, ph, k: (bi, cidx(ph, k), 0)),
        pl.BlockSpec((1, CHUNK, G_B * N_STATE), lambda bi, ph, k: (bi, cidx(ph, k), D_INNER // (G_B * N_STATE))),
        pl.BlockSpec((1, CHUNK, G_B * N_STATE), lambda bi, ph, k: (bi, cidx(ph, k), D_INNER // (G_B * N_STATE) + 1)),
        pl.BlockSpec((1, CHUNK, LANE), lambda bi, ph, k: (bi, cidx(ph, k), dt_col)),
        pl.BlockSpec((1, LANE), lambda bi, ph, k: (0, 0)),
        pl.BlockSpec((1, LANE), lambda bi, ph, k: (0, 0)),
        pl.BlockSpec((1, D_INNER), lambda bi, ph, k: (0, 0)),
    ]
    args = [xbc, xbc, xbc, dt_src, dtb, alog, dsk]
    if has_h0:
        in_specs.append(pl.BlockSpec((1, 2, D_INNER, N_STATE), lambda bi, ph, k: (bi, 0, 0, 0)))
        args.append(h0)
    return pl.pallas_call(
        functools.partial(_ssd_kernel, nc=nc, has_h0=has_h0),
        grid=(b, 2, nc),
        in_specs=in_specs,
        out_specs=[
            pl.BlockSpec((1, CHUNK, D_INNER), lambda bi, ph, k: (bi, ph * k, 0)),
            pl.BlockSpec((1, 2, D_INNER, N_STATE), lambda bi, ph, k: (bi, 0, 0, 0)),
        ],
        out_shape=[
            jax.ShapeDtypeStruct((b, seq, D_INNER), F32),
            jax.ShapeDtypeStruct((b, 2, D_INNER, N_STATE), F32),
        ],
        scratch_shapes=[
            pltpu.VMEM((D_INNER, N_STATE), F32),
            pltpu.VMEM((nc, D_INNER, N_STATE), F32),
        ],
        compiler_params=_cparams("parallel", "arbitrary", "arbitrary"),
        name="ssd",
    )(*args)


def _post_kernel(oa_ref, y_ref, z_ref, oc_ref, gate_ref, x_ref, mod_ref, gb_ref, g2_ref,
                 wa_ref, wb_ref, wc_ref, wo_ref, xo_ref, h2_ref, *, n_sub):
    sub = x_ref.shape[0] // n_sub
    for r in range(n_sub):
        rows = slice(r * sub, (r + 1) * sub)
        yb = _rms(y_ref[rows, :] * _silu(z_ref[rows, :].astype(F32)), gb_ref[...]).astype(BF16)
        out_a = jnp.dot(oa_ref[rows, :], wa_ref[...], preferred_element_type=F32)
        out_b = jnp.dot(yb, wb_ref[...], preferred_element_type=F32)
        out_c = jnp.dot(oc_ref[rows, :], wc_ref[...], preferred_element_type=F32)
        merged = (jax.nn.sigmoid(gate_ref[rows, 0:D_MODEL].astype(F32)) * out_a
                  + jax.nn.sigmoid(gate_ref[rows, D_MODEL:2 * D_MODEL].astype(F32)) * out_b
                  + jax.nn.sigmoid(gate_ref[rows, 2 * D_MODEL:3 * D_MODEL].astype(F32)) * out_c)
        out = jnp.dot(merged.astype(BF16), wo_ref[...], preferred_element_type=F32)
        xn = x_ref[rows, :] + mod_ref[0, 2:3, :] * out
        xo_ref[rows, :] = xn
        h2 = _rms(xn, g2_ref[...]) * (1.0 + mod_ref[0, 4:5, :]) + mod_ref[0, 3:4, :]
        h2_ref[rows, :] = h2.astype(BF16)


POST_SUB = 2


def _post_call(oa, y, proj, oc, x2d, mod, gb, g2, wa, wb, wc, wo, row_fn, tm):
    t = x2d.shape[0]
    full = lambda i: (0, 0)
    return pl.pallas_call(
        functools.partial(_post_kernel, n_sub=POST_SUB),
        grid=(t // tm,),
        in_specs=[
            pl.BlockSpec((tm, H_A * V_A), lambda i: (i, 0)),
            pl.BlockSpec((tm, D_INNER), lambda i: (i, 0)),
            pl.BlockSpec((tm, D_INNER), lambda i: (i, C_Z // D_INNER)),
            pl.BlockSpec((tm, H_C * DV_C), lambda i: (i, 0)),
            pl.BlockSpec((tm, 3 * D_MODEL), lambda i: (i, 0)),
            pl.BlockSpec((tm, D_MODEL), lambda i: (i, 0)),
            pl.BlockSpec((1, N_MOD, D_MODEL), lambda i: (row_fn(i, tm), 0, 0)),
            pl.BlockSpec((1, D_INNER), full),
            pl.BlockSpec((1, D_MODEL), full),
            pl.BlockSpec((H_A * V_A, D_MODEL), full),
            pl.BlockSpec((D_INNER, D_MODEL), full),
            pl.BlockSpec((H_C * DV_C, D_MODEL), full),
            pl.BlockSpec((D_MODEL, D_MODEL), full),
        ],
        out_specs=[
            pl.BlockSpec((tm, D_MODEL), lambda i: (i, 0)),
            pl.BlockSpec((tm, D_MODEL), lambda i: (i, 0)),
        ],
        out_shape=[
            jax.ShapeDtypeStruct((t, D_MODEL), F32),
            jax.ShapeDtypeStruct((t, D_MODEL), BF16),
        ],
        compiler_params=_cparams("parallel"),
        name="post_mix",
    )(oa, y, proj, oc, proj, x2d, mod, gb, g2, wa, wb, wc, wo)


FFN_CHUNK = 256
FFN_DOWN_PARTS = 2


def _ffn_kernel(h_ref, prev_ref, next_ref, wu_ref, cw_ref, cb_ref, wd_ref, x_ref, mod_ref, o_ref, a_scr, *, tm, seq):
    i = pl.program_id(0)
    ext = tm + 2 * HALO
    if tm > seq:
        pos = (i * tm + lax.broadcasted_iota(jnp.int32, (tm, 1), 0)) % seq
        has_prev = pos != 0
        has_next = pos != seq - 1
        hext = jnp.concatenate([prev_ref[...], h_ref[...], next_ref[...]], axis=0)
    else:
        keep_prev = jnp.where((i * tm) % seq == 0, 0.0, 1.0).astype(BF16)
        keep_next = jnp.where(((i + 1) * tm) % seq == 0, 0.0, 1.0).astype(BF16)
        hext = jnp.concatenate([prev_ref[...] * keep_prev, h_ref[...], next_ref[...] * keep_next], axis=0)

    def conv(u, c0):
        up = pltpu.roll(u, 1, 0)[HALO:HALO + tm]
        un = pltpu.roll(u, ext - 1, 0)[HALO:HALO + tm]
        if tm > seq:
            up = jnp.where(has_prev, up, 0.0)
            un = jnp.where(has_next, un, 0.0)
        cols = slice(c0, c0 + FFN_CHUNK)
        return (cw_ref[0:1, cols] * up + cw_ref[1:2, cols] * u[HALO:HALO + tm]
                + cw_ref[2:3, cols] * un + cb_ref[:, cols])

    n_chunks = D_FF // FFN_CHUNK
    ends = [(n_chunks * (p + 1)) // FFN_DOWN_PARTS * FFN_CHUNK for p in range(FFN_DOWN_PARTS)]
    out = None
    start = 0
    for c in range(n_chunks):
        g0, v0 = c * FFN_CHUNK, D_FF + c * FFN_CHUNK
        ug = jnp.dot(hext, wu_ref[:, g0:g0 + FFN_CHUNK], preferred_element_type=F32)
        uv = jnp.dot(hext, wu_ref[:, v0:v0 + FFN_CHUNK], preferred_element_type=F32)
        a_scr[:, g0:g0 + FFN_CHUNK] = (_silu(conv(ug, g0)) * conv(uv, v0)).astype(BF16)
        end = g0 + FFN_CHUNK
        if end in ends:
            part = jnp.dot(a_scr[:, start:end], wd_ref[start:end, :], preferred_element_type=F32)
            out = part if out is None else out + part
            start = end
    o_ref[...] = x_ref[...] + mod_ref[0, 5:6, :] * out


def _ffn_call(h2, wu, cw, cb, wd, x2d, mod, row_fn, seq, tm):
    t = h2.shape[0]
    n = 2 * D_FF
    full = lambda i: (0, 0)
    return pl.pallas_call(
        functools.partial(_ffn_kernel, tm=tm, seq=seq),
        grid=(t // tm,),
        in_specs=_halo_specs(tm, D_MODEL, 0, t) + [
            pl.BlockSpec((D_MODEL, n), full),
            pl.BlockSpec((3, n), full),
            pl.BlockSpec((1, n), full),
            pl.BlockSpec((D_FF, D_MODEL), full),
            pl.BlockSpec((tm, D_MODEL), lambda i: (i, 0)),
            pl.BlockSpec((1, N_MOD, D_MODEL), lambda i: (row_fn(i, tm), 0, 0)),
        ],
        out_specs=pl.BlockSpec((tm, D_MODEL), lambda i: (i, 0)),
        out_shape=jax.ShapeDtypeStruct((t, D_MODEL), F32),
        scratch_shapes=[pltpu.VMEM((tm, D_FF), BF16)],
        compiler_params=_cparams("parallel"),
        name="ffn",
    )(h2, h2, h2, wu, cw, cb, wd, x2d, mod)


def _final_norm_kernel(x_ref, g_ref, o_ref):
    o_ref[...] = _rms(x_ref[...], g_ref[...])


def _final_norm_call(x2d, g, tm):
    t = x2d.shape[0]
    return pl.pallas_call(
        _final_norm_kernel,
        grid=(t // tm,),
        in_specs=[pl.BlockSpec((tm, D_MODEL), lambda i: (i, 0)), pl.BlockSpec((1, D_MODEL), lambda i: (0, 0))],
        out_specs=pl.BlockSpec((tm, D_MODEL), lambda i: (i, 0)),
        out_shape=jax.ShapeDtypeStruct((t, D_MODEL), F32),
        compiler_params=_cparams("parallel"),
        name="final_norm",
    )(x2d, g)


def _rope_tables(n_tokens, rot_dim, lead, tail):
    rows = n_tokens // GRID_W
    row = jnp.broadcast_to(jnp.arange(rows, dtype=F32)[:, None], (rows, GRID_W)).reshape(-1)
    col = jnp.broadcast_to(jnp.arange(GRID_W, dtype=F32)[None, :], (rows, GRID_W)).reshape(-1)
    n_freq = rot_dim // 4
    freqs = ROPE_THETA ** (-jnp.arange(n_freq, dtype=F32) / n_freq)
    ang = jnp.concatenate([row[:, None] * freqs, col[:, None] * freqs], axis=-1)
    cos, sin = jnp.cos(ang), jnp.sin(ang)
    c = jnp.repeat(cos, 2, axis=-1)
    s = jnp.stack([-sin, sin], axis=-1).reshape(n_tokens, rot_dim)
    reps = (LANE - lead - tail) // rot_dim
    c = jnp.tile(c, (1, reps))
    s = jnp.tile(s, (1, reps))
    c = jnp.concatenate([jnp.ones((n_tokens, lead), F32), c, jnp.ones((n_tokens, tail), F32)], axis=-1)
    s = jnp.concatenate([jnp.zeros((n_tokens, lead), F32), s, jnp.zeros((n_tokens, tail), F32)], axis=-1)
    return c, s


def _pad_w_in(w):
    cq, ckv, krope, z, xbc, dt, qc, kc, vc, gate = jnp.split(
        w, [384, 640, 672, 1696, 3744, 3776, 4288, 4800, 5312], axis=-1)
    d = w.shape[0]
    zeros = lambda n: jnp.zeros((d, n), w.dtype)
    kr_pad = jnp.concatenate([zeros(NOPE_A), krope, zeros(LANE - NOPE_A - ROPE_A)], axis=-1)
    dt_pad = jnp.concatenate([dt, zeros(LANE - 2 * H_B)], axis=-1)
    main = jnp.concatenate([gate, z, xbc, qc, kc, vc, cq, kr_pad, ckv], axis=-1)
    side_ctx = jnp.concatenate([kc, vc, ckv, kr_pad, dt_pad], axis=-1)
    return main.astype(BF16), side_ctx.astype(BF16), dt_pad.astype(BF16)


S_KC, S_VC, S_CKV, S_KROPE, S_DT = 0, 512, 1024, 1280, 1408


def _prep_layer(l, w):
    d_q = NOPE_A + ROPE_A
    w_uq = w['w_uq'][l].reshape(Q_RANK, H_A, d_q)
    w_uq = jnp.pad(w_uq, ((0, 0), (0, 0), (0, LANE - d_q))).reshape(Q_RANK, H_A * LANE)
    w_ukv = w['w_ukv'][l].reshape(KV_RANK, H_A, NOPE_A + V_A)
    wk = jnp.pad(w_ukv[:, :, :NOPE_A], ((0, 0), (0, 0), (0, LANE - NOPE_A))).reshape(KV_RANK, H_A * LANE)
    wv = w_ukv[:, :, NOPE_A:].reshape(KV_RANK, H_A // 2, 2, V_A)
    zv = jnp.zeros_like(wv[:, :, 0])
    wv = jnp.stack([jnp.concatenate([wv[:, :, 0], zv], -1), jnp.concatenate([zv, wv[:, :, 1]], -1)],
                   axis=2).reshape(KV_RANK, H_A * LANE)
    pad_lane = lambda v: jnp.pad(v.reshape(1, -1), ((0, 0), (0, LANE - v.size)))
    w_in, w_side_ctx, w_side_lat = _pad_w_in(w['w_in'][l])
    return dict(
        w_in=w_in, w_side_ctx=w_side_ctx, w_side_lat=w_side_lat,
        norm1_g=w['norm1_g'][l].reshape(1, -1),
        norm2_g=w['norm2_g'][l].reshape(1, -1),
        q_norm_g=w['q_norm_g'][l].reshape(1, -1),
        kv_norm_g=w['kv_norm_g'][l].reshape(1, -1),
        w_uq=w_uq.astype(BF16),
        w_ukv=jnp.concatenate([wk, wv], axis=-1).astype(BF16),
        w_oA=w['w_oA'][l].astype(BF16),
        w_oB=w['w_oB'][l].astype(BF16),
        w_oC=w['w_oC'][l].astype(BF16),
        w_out=w['w_out'][l].astype(BF16),
        ssd_conv_w=w['ssd_conv_w'][l],
        ssd_conv_b=w['ssd_conv_b'][l].reshape(1, -1),
        dt_bias=pad_lane(w['ssd_dt_bias'][l]),
        a_log=pad_lane(w['ssd_a_log'][l]),
        d_skip=jnp.repeat(w['ssd_d_skip'][l], P_B).reshape(1, -1),
        ssd_norm_g=w['ssd_norm_g'][l].reshape(1, -1),
        diff_lambda=w['diff_lambda'][l],
        diff_norm_g=w['diff_norm_g'][l].reshape(1, -1),
        ffn_w_up=w['ffn_w_up'][l].astype(BF16),
        ffn_conv_w=w['ffn_conv_w'][l],
        ffn_conv_b=w['ffn_conv_b'][l].reshape(1, -1),
        ffn_w_down=w['ffn_w_down'][l].astype(BF16),
    )


def _tile(n, pref):
    t = min(n, pref)
    while n % t:
        t //= 2
    return t


def _block(x, mod, lp, lam_init, row_fn, tabs, ctx):
    b, seq, _ = x.shape
    t = b * seq
    x2d = x.reshape(t, D_MODEL)
    is_ctx = ctx is None
    tm_big = _tile(t if is_ctx else seq, 1024)
    tm = _tile(seq, 256)
    proj, side = _inproj_call(x2d, lp['norm1_g'], mod, lp['w_in'],
                              lp['w_side_ctx'] if is_ctx else lp['w_side_lat'], row_fn, tm_big)
    proj3 = proj.reshape(b, seq, N_PROJ)
    tab_a, tab_c = tabs if tabs is not None else (None, None)
    qa = _mla_q_call(proj, lp['q_norm_g'], lp['w_uq'], tab_a, seq, tm_big)
    if is_ctx:
        k_new, v_new, ckv_n = _kv_up_call(side, S_CKV // KV_RANK, side, S_KROPE // LANE, lp['kv_norm_g'],
                                          lp['w_ukv'], None, seq, tm_big, want_ckv=True)
    else:
        k_new, v_new = _kv_up_call(proj, C_CKV // KV_RANK, proj, C_KROPE // LANE, lp['kv_norm_g'], lp['w_ukv'],
                                   tab_a, seq, tm_big)
    wide = H_A * LANE
    pieces_a = [(k_new.reshape(b, seq, wide), 0, v_new.reshape(b, seq, wide), 0)]
    qd, kd = _dprep_call(proj, tab_c, seq, tm_big)
    pieces_c = [(kd.reshape(b, seq, H_C * LANE), 0, proj3, C_VC // LANE)]
    h0 = None
    if not is_ctx:
        ctx_ckv, ctx_krope, ctx_k, ctx_v, h0 = ctx
        past = ctx_ckv.shape[1]
        kr_pad = jnp.pad(ctx_krope.reshape(b * past, ROPE_A), ((0, 0), (NOPE_A, LANE - NOPE_A - ROPE_A)))
        k_c, v_c = _kv_up_call(ctx_ckv.reshape(b * past, KV_RANK), 0, kr_pad, 0, None, lp['w_ukv'],
                               None, past, _tile(b * past, 1024))
        pieces_a = [(k_c.reshape(b, past, wide), 0, v_c.reshape(b, past, wide), 0)] + pieces_a
        pieces_c = [(ctx_k.reshape(b, past, H_C * LANE), 0, ctx_v.reshape(b, past, H_C * DV_C), 0)] + pieces_c
        h0 = h0.reshape(b, 2, D_INNER, N_STATE)
    oa = _attn_call(qa.reshape(b, seq, wide), pieces_a, H_A, False, _tile(seq, 1024), H_A if is_ctx else 2)
    oc = _attn_call(qd.reshape(b, seq, H_C * LANE), pieces_c, H_C, True, _tile(seq, 512), H_C,
                    extra=(lp['diff_lambda'], lp['diff_norm_g']), lam_init=lam_init)
    xbc = _ssd_conv_call(proj, lp['ssd_conv_w'], lp['ssd_conv_b'], seq, tm)
    y, hfin = _ssd_call(xbc.reshape(b, seq, CONV_CH), side.reshape(b, seq, side.shape[1]),
                        S_DT // LANE if is_ctx else 0, lp['dt_bias'], lp['a_log'], lp['d_skip'], h0)
    x2d, h2 = _post_call(oa.reshape(t, H_A * V_A), y.reshape(t, D_INNER), proj, oc.reshape(t, H_C * DV_C), x2d, mod,
                         lp['ssd_norm_g'], lp['norm2_g'], lp['w_oA'], lp['w_oB'], lp['w_oC'], lp['w_out'],
                         row_fn, _tile(t if is_ctx else seq, 512))
    x2d = _ffn_call(h2, lp['ffn_w_up'], lp['ffn_conv_w'], lp['ffn_conv_b'], lp['ffn_w_down'], x2d, mod, row_fn,
                    seq, _tile(t if is_ctx else seq, 1024))
    st = None
    if is_ctx:
        st = (ckv_n.reshape(b, seq, KV_RANK),
              side[:, S_KROPE + NOPE_A:S_KROPE + NOPE_A + ROPE_A].reshape(b, seq, ROPE_A),
              side[:, S_KC:S_KC + H_C * 2 * DH_C].reshape(b, seq, H_C, 2, DH_C),
              side[:, S_VC:S_VC + H_C * DV_C].reshape(b, seq, H_C, DV_C),
              hfin.reshape(b, 2, H_B, P_B, N_STATE))
    return x2d.reshape(b, seq, D_MODEL), st


def kernel(x_prompt, x_sample, cache_mla_ckv, cache_mla_krope, cache_diff_k, cache_diff_v, state_ssd, c, c_ctx, w_mod, b_mod, norm1_g, norm2_g, w_in, q_norm_g, kv_norm_g, w_uq, w_ukv, w_oA, ssd_conv_w, ssd_conv_b, ssd_dt_bias, ssd_a_log, ssd_d_skip, ssd_norm_g, w_oB, diff_lambda, diff_norm_g, w_oC, w_out, ffn_w_up, ffn_conv_w, ffn_conv_b, ffn_w_down, final_norm_g):
    weights = dict(norm1_g=norm1_g, norm2_g=norm2_g, w_in=w_in, q_norm_g=q_norm_g, kv_norm_g=kv_norm_g,
                   w_uq=w_uq, w_ukv=w_ukv, w_oA=w_oA, ssd_conv_w=ssd_conv_w, ssd_conv_b=ssd_conv_b,
                   ssd_dt_bias=ssd_dt_bias, ssd_a_log=ssd_a_log, ssd_d_skip=ssd_d_skip, ssd_norm_g=ssd_norm_g,
                   w_oB=w_oB, diff_lambda=diff_lambda, diff_norm_g=diff_norm_g, w_oC=w_oC, w_out=w_out,
                   ffn_w_up=ffn_w_up, ffn_conv_w=ffn_conv_w, ffn_conv_b=ffn_conv_b, ffn_w_down=ffn_w_down)
    n_dec, n_lat = x_sample.shape[0], x_sample.shape[1]
    ctx_row = n_dec
    cond = jnp.concatenate([c, c_ctx[None, :], jnp.zeros((COND_ROWS - n_dec - 1, D_MODEL), F32)], axis=0)
    mod_all = _mod_call(cond, w_mod, b_mod).reshape(DEPTH, COND_ROWS, N_MOD, D_MODEL)
    tabs = (_rope_tables(n_lat, ROPE_A, NOPE_A, LANE - NOPE_A - ROPE_A), _rope_tables(n_lat, DH_C, 0, 0))
    lat_row = lambda i, tm: (i * tm) // n_lat
    ctx_row_fn = lambda i, tm: ctx_row
    xp, xs = x_prompt, x_sample
    new = [[] for _ in range(5)]
    for l in range(DEPTH):
        lp = _prep_layer(l, weights)
        lam_init = 0.8 - 0.6 * math.exp(-0.3 * l)
        xp, st = _block(xp, mod_all[l], lp, lam_init, ctx_row_fn, None, None)
        for acc, v in zip(new, st):
            acc.append(v)
        ctx = (cache_mla_ckv[:, l], cache_mla_krope[:, l], cache_diff_k[:, l], cache_diff_v[:, l], state_ssd[:, l])
        xs, _ = _block(xs, mod_all[l], lp, lam_init, lat_row, tabs, ctx)
    bp, sp = xp.shape[0], xp.shape[1]
    y_prompt = _final_norm_call(xp.reshape(bp * sp, D_MODEL), final_norm_g.reshape(1, -1), _tile(bp * sp, 1024))
    y_sample = _final_norm_call(xs.reshape(n_dec * n_lat, D_MODEL), final_norm_g.reshape(1, -1),
                                _tile(n_dec * n_lat, 1024))
    return (y_prompt.reshape(bp, sp, D_MODEL), y_sample.reshape(n_dec, n_lat, D_MODEL),
            jnp.stack(new[0], axis=1), jnp.stack(new[1], axis=1), jnp.stack(new[2], axis=1),
            jnp.stack(new[3], axis=1), jnp.stack(new[4], axis=1))
```

```python
import functools
import math

import jax
import jax.numpy as jnp
from jax import lax
from jax.experimental import pallas as pl
from jax.experimental.pallas import tpu as pltpu

F32 = jnp.float32
BF16 = jnp.bfloat16

D_MODEL = 1024
DEPTH = 4
GRID_W = 64
ROPE_THETA = 10000.0
EPS = 1e-6
H_A, NOPE_A, ROPE_A, V_A = 8, 64, 32, 64
Q_RANK, KV_RANK = 384, 256
H_B, P_B, N_STATE, G_B, CHUNK = 16, 64, 128, 4, 128
D_INNER = H_B * P_B
CONV_CH = D_INNER + 2 * G_B * N_STATE
H_C, DH_C = 4, 64
DV_C = 2 * DH_C
D_FF = 2816
N_MOD = 6
COND_ROWS = 16

LOG2E = math.log2(math.e)
LANE = 128
SUBLANE = 8
VMEM_LIMIT = 56 * 1024 * 1024

C_GATE = 0
C_Z = 3072
C_XBC = 4096
C_QC = 6144
C_KC = 6656
C_VC = 7168
C_CQ = 7680
C_KROPE = 8064
C_CKV = 8192
N_PROJ = 8448
PROJ_TILE = 2816


def _cparams(*sem):
    return pltpu.CompilerParams(dimension_semantics=sem, vmem_limit_bytes=VMEM_LIMIT)


def _silu(x):
    return x * jax.nn.sigmoid(x)


def _rms(x, g):
    ms = jnp.mean(x * x, axis=-1, keepdims=True)
    return x * lax.rsqrt(ms + EPS) * g


def _rope(x, c, s):
    n = x.shape[-1]
    lane = lax.broadcasted_iota(jnp.int32, x.shape, x.ndim - 1)
    nxt = pltpu.roll(x, n - 1, x.ndim - 1)
    prv = pltpu.roll(x, 1, x.ndim - 1)
    return x * c + jnp.where(lane % 2 == 0, nxt, prv) * s


def _mod_kernel(c_ref, w_ref, b_ref, o_ref):
    s = _silu(c_ref[...]).astype(BF16)
    o_ref[0] = jnp.dot(s, w_ref[0].astype(BF16), preferred_element_type=F32) + b_ref[0]


def _mod_call(cond, w_mod, b_mod):
    tn = 1536
    n = N_MOD * D_MODEL
    return pl.pallas_call(
        _mod_kernel,
        grid=(DEPTH, n // tn),
        in_specs=[
            pl.BlockSpec((COND_ROWS, D_MODEL), lambda l, j: (0, 0)),
            pl.BlockSpec((1, D_MODEL, tn), lambda l, j: (l, 0, j)),
            pl.BlockSpec((1, 1, tn), lambda l, j: (l, 0, j)),
        ],
        out_specs=pl.BlockSpec((1, COND_ROWS, tn), lambda l, j: (l, 0, j)),
        out_shape=jax.ShapeDtypeStruct((DEPTH, COND_ROWS, n), F32),
        compiler_params=_cparams("parallel", "parallel"),
        name="mod",
    )(cond, w_mod, b_mod.reshape(DEPTH, 1, n))


INPROJ_SUB = 4


def _inproj_kernel(x_ref, g_ref, mod_ref, w_ref, ws_ref, o_ref, os_ref, h_scr):
    j = pl.program_id(1)

    @pl.when(j == 0)
    def _():
        sub = x_ref.shape[0] // INPROJ_SUB
        for r in range(INPROJ_SUB):
            rows = slice(r * sub, (r + 1) * sub)
            h = _rms(x_ref[rows, :], g_ref[...]) * (1.0 + mod_ref[0, 1:2, :]) + mod_ref[0, 0:1, :]
            hb = h.astype(BF16)
            h_scr[rows, :] = hb
            os_ref[rows, :] = jnp.dot(hb, ws_ref[...], preferred_element_type=F32)
            o_ref[rows, :] = jnp.dot(hb, w_ref[...], preferred_element_type=F32).astype(BF16)

    @pl.when(j > 0)
    def _():
        o_ref[...] = jnp.dot(h_scr[...], w_ref[...], preferred_element_type=F32).astype(BF16)


def _inproj_call(x2d, g, mod, w, w_side, row_fn, tm):
    t = x2d.shape[0]
    tn = PROJ_TILE
    n_side = w_side.shape[1]
    return pl.pallas_call(
        _inproj_kernel,
        grid=(t // tm, N_PROJ // tn),
        in_specs=[
            pl.BlockSpec((tm, D_MODEL), lambda i, j: (i, 0)),
            pl.BlockSpec((1, D_MODEL), lambda i, j: (0, 0)),
            pl.BlockSpec((1, N_MOD, D_MODEL), lambda i, j: (row_fn(i, tm), 0, 0)),
            pl.BlockSpec((D_MODEL, tn), lambda i, j: (0, j)),
            pl.BlockSpec((D_MODEL, n_side), lambda i, j: (0, 0)),
        ],
        out_specs=[
            pl.BlockSpec((tm, tn), lambda i, j: (i, j)),
            pl.BlockSpec((tm, n_side), lambda i, j: (i, 0)),
        ],
        out_shape=[
            jax.ShapeDtypeStruct((t, N_PROJ), BF16),
            jax.ShapeDtypeStruct((t, n_side), F32),
        ],
        scratch_shapes=[pltpu.VMEM((tm, D_MODEL), BF16)],
        compiler_params=_cparams("parallel", "arbitrary"),
        name="inproj",
    )(x2d, g, mod, w, w_side)


def _mla_q_kernel(*refs, rope, scale):
    if rope:
        p_ref, g_ref, w_ref, c_ref, s_ref, o_ref = refs
    else:
        p_ref, g_ref, w_ref, o_ref = refs
    cqn = _rms(p_ref[:, :Q_RANK].astype(F32), g_ref[...]).astype(BF16)
    q = jnp.dot(cqn, w_ref[...], preferred_element_type=F32)
    for h in range(H_A):
        qh = q[:, h * LANE:(h + 1) * LANE]
        if rope:
            qh = _rope(qh, c_ref[...], s_ref[...])
        o_ref[:, h * LANE:(h + 1) * LANE] = (qh * scale).astype(BF16)


def _mla_q_call(proj, g, w, tabs, seq, tm):
    t = proj.shape[0]
    rope = tabs is not None
    nb = seq // tm
    in_specs = [
        pl.BlockSpec((tm, 512), lambda i: (i, C_CQ // 512)),
        pl.BlockSpec((1, Q_RANK), lambda i: (0, 0)),
        pl.BlockSpec((Q_RANK, H_A * LANE), lambda i: (0, 0)),
    ]
    args = [proj, g, w]
    if rope:
        in_specs += [pl.BlockSpec((tm, LANE), lambda i: (i % nb, 0))] * 2
        args += list(tabs)
    return pl.pallas_call(
        functools.partial(_mla_q_kernel, rope=rope, scale=LOG2E * (NOPE_A + ROPE_A) ** -0.5),
        grid=(t // tm,),
        in_specs=in_specs,
        out_specs=pl.BlockSpec((tm, H_A * LANE), lambda i: (i, 0)),
        out_shape=jax.ShapeDtypeStruct((t, H_A * LANE), BF16),
        compiler_params=_cparams("parallel"),
        name="mla_q",
    )(*args)


def _kv_up_kernel(*refs, norm, rope, want_ckv):
    refs = list(refs)
    ckv_ref, kr_ref = refs[0], refs[1]
    pos = 2
    if norm:
        g_ref = refs[pos]
        pos += 1
    w_ref = refs[pos]
    pos += 1
    if rope:
        c_ref, s_ref = refs[pos], refs[pos + 1]
        pos += 2
    k_ref, v_ref = refs[pos], refs[pos + 1]
    ckv = ckv_ref[...].astype(F32)
    if norm:
        ckv = _rms(ckv, g_ref[...])
        if want_ckv:
            refs[pos + 2][...] = ckv
    kv = jnp.dot(ckv.astype(BF16), w_ref[...], preferred_element_type=F32)
    kr = kr_ref[...].astype(F32)
    if rope:
        kr = _rope(kr, c_ref[...], s_ref[...])
    for h in range(H_A):
        k_ref[:, h * LANE:(h + 1) * LANE] = (kv[:, h * LANE:(h + 1) * LANE] + kr).astype(BF16)
    v_ref[...] = kv[:, H_A * LANE:].astype(BF16)


def _kv_up_call(ckv_src, ckv_col, kr_src, kr_col, g, w, tabs, seq, tm, want_ckv=False):
    t = ckv_src.shape[0]
    norm = g is not None
    rope = tabs is not None
    nb = seq // tm
    in_specs = [
        pl.BlockSpec((tm, KV_RANK), lambda i: (i, ckv_col)),
        pl.BlockSpec((tm, LANE), lambda i: (i, kr_col)),
    ]
    args = [ckv_src, kr_src]
    if norm:
        in_specs.append(pl.BlockSpec((1, KV_RANK), lambda i: (0, 0)))
        args.append(g)
    in_specs.append(pl.BlockSpec((KV_RANK, 2 * H_A * LANE), lambda i: (0, 0)))
    args.append(w)
    if rope:
        in_specs += [pl.BlockSpec((tm, LANE), lambda i: (i % nb, 0))] * 2
        args += list(tabs)
    out_specs = [pl.BlockSpec((tm, H_A * LANE), lambda i: (i, 0))] * 2
    out_shape = [jax.ShapeDtypeStruct((t, H_A * LANE), BF16)] * 2
    if want_ckv:
        out_specs.append(pl.BlockSpec((tm, KV_RANK), lambda i: (i, 0)))
        out_shape.append(jax.ShapeDtypeStruct((t, KV_RANK), F32))
    return pl.pallas_call(
        functools.partial(_kv_up_kernel, norm=norm, rope=rope, want_ckv=want_ckv),
        grid=(t // tm,),
        in_specs=in_specs,
        out_specs=out_specs,
        out_shape=out_shape,
        compiler_params=_cparams("parallel"),
        name="kv_up",
    )(*args)


def _dprep_kernel(*refs, rope):
    if rope:
        q_ref, k_ref, c_ref, s_ref, qo_ref, ko_ref = refs
    else:
        q_ref, k_ref, qo_ref, ko_ref = refs
    for b in range(H_C):
        q = q_ref[:, b * LANE:(b + 1) * LANE].astype(F32)
        if rope:
            q = _rope(q, c_ref[...], s_ref[...])
        qo_ref[:, b * LANE:(b + 1) * LANE] = (q * (LOG2E * DH_C ** -0.5)).astype(BF16)
    for b in range(H_C):
        k = k_ref[:, b * LANE:(b + 1) * LANE].astype(F32)
        if rope:
            k = _rope(k, c_ref[...], s_ref[...])
        ko_ref[:, b * LANE:(b + 1) * LANE] = k.astype(BF16)


def _dprep_call(proj, tabs, seq, tm):
    t = proj.shape[0]
    rope = tabs is not None
    nb = seq // tm
    in_specs = [
        pl.BlockSpec((tm, 512), lambda i: (i, C_QC // 512)),
        pl.BlockSpec((tm, 512), lambda i: (i, C_KC // 512)),
    ]
    args = [proj, proj]
    if rope:
        in_specs += [pl.BlockSpec((tm, LANE), lambda i: (i % nb, 0))] * 2
        args += list(tabs)
    return pl.pallas_call(
        functools.partial(_dprep_kernel, rope=rope),
        grid=(t // tm,),
        in_specs=in_specs,
        out_specs=[
            pl.BlockSpec((tm, 512), lambda i: (i, 0)),
            pl.BlockSpec((tm, 512), lambda i: (i, 0)),
        ],
        out_shape=[
            jax.ShapeDtypeStruct((t, 512), BF16),
            jax.ShapeDtypeStruct((t, 512), BF16),
        ],
        compiler_params=_cparams("parallel"),
        name="dprep",
    )(*args)


def _attn_kernel(*refs, chunks, n_pieces, diff, tq, hp, lam_init):
    q_ref = refs[0]
    kv = refs[1:1 + 2 * n_pieces]
    o_ref = refs[-1]
    nt = (((1,), (1,)), ((), ()))

    def lane_sum(p):
        parts = [p[:, c * LANE:(c + 1) * LANE] for c in range(p.shape[1] // LANE)]
        while len(parts) > 1:
            parts = [a + b for a, b in zip(parts[::2], parts[1::2])] + ([parts[-1]] if len(parts) % 2 else [])
        return parts[0]

    def one_head(h):
        cols = slice(h * LANE, (h + 1) * LANE)
        q = q_ref[0, :, cols]
        if diff:
            lane = lax.broadcasted_iota(jnp.int32, q.shape, 1)
            zero = jnp.zeros_like(q)
            q = jnp.concatenate([jnp.where(lane < DH_C, q, zero), jnp.where(lane >= DH_C, q, zero)], axis=0)
        m = l = acc = None
        for pi, st, sz in chunks:
            k = kv[2 * pi][0, st:st + sz, cols].astype(BF16)
            v = kv[2 * pi + 1][0, st:st + sz, cols].astype(BF16)
            s = lax.dot_general(q, k, nt, preferred_element_type=F32)
            ms = s.max(axis=-1, keepdims=True)
            if m is None:
                m = ms
                p = jnp.exp2(s - m)
                l = lane_sum(p)
                acc = jnp.dot(p.astype(BF16), v, preferred_element_type=F32)
            else:
                m_new = jnp.maximum(m, ms)
                alpha = jnp.exp2(m - m_new)
                p = jnp.exp2(s - m_new)
                l = alpha * l + lane_sum(p)
                acc = alpha * acc + jnp.dot(p.astype(BF16), v, preferred_element_type=F32)
                m = m_new
        return acc / l.sum(axis=-1, keepdims=True)

    if diff:
        lam_ref, g_ref = refs[1 + 2 * n_pieces], refs[2 + 2 * n_pieces]
        lv = lam_ref[...]
        lam = (jnp.exp(jnp.sum(lv[0:1] * lv[1:2], axis=-1, keepdims=True))
               - jnp.exp(jnp.sum(lv[2:3] * lv[3:4], axis=-1, keepdims=True)) + lam_init)
        for h in range(hp):
            o = one_head(h)
            oc = o[:tq] - lam * o[tq:]
            o_ref[0, :, h * LANE:(h + 1) * LANE] = (_rms(oc, g_ref[...]) * (1.0 - lam_init)).astype(BF16)
    else:
        for h in range(0, hp, 2):
            o_ref[0, :, (h // 2) * LANE:(h // 2 + 1) * LANE] = (one_head(h) + one_head(h + 1)).astype(BF16)


KEY_CHUNK = 256


def _attn_call(q, pieces, n_heads, diff, tq, hp, extra=(), lam_init=0.0):
    b, seq = q.shape[0], q.shape[1]
    ow = hp * LANE if diff else hp * V_A
    in_specs = [pl.BlockSpec((1, tq, hp * LANE), lambda bi, h, qi: (bi, qi, h))]
    args = [q]
    chunks = []
    for pi, (k, kc, v, vc) in enumerate(pieces):
        s = k.shape[1]
        in_specs += [pl.BlockSpec((1, s, hp * LANE), lambda bi, h, qi, kc=kc: (bi, 0, kc // hp + h)),
                     pl.BlockSpec((1, s, hp * LANE), lambda bi, h, qi, vc=vc: (bi, 0, vc // hp + h))]
        args += [k, v]
        step = _tile(s, KEY_CHUNK)
        chunks += [(pi, st, step) for st in range(0, s, step)]
    for e in extra:
        in_specs.append(pl.BlockSpec(e.shape, lambda bi, h, qi: (0, 0)))
        args.append(e)
    return pl.pallas_call(
        functools.partial(_attn_kernel, chunks=tuple(chunks), n_pieces=len(pieces), diff=diff, tq=tq, hp=hp,
                          lam_init=lam_init),
        grid=(b, n_heads // hp, seq // tq),
        in_specs=in_specs,
        out_specs=pl.BlockSpec((1, tq, ow), lambda bi, h, qi: (bi, qi, h)),
        out_shape=jax.ShapeDtypeStruct((b, seq, (n_heads // hp) * ow), BF16),
        compiler_params=_cparams("parallel", "parallel", "arbitrary"),
        name="diff_attn" if diff else "mla_attn",
    )(*args)


def _conv3(x, prev_ref, next_ref, w_ref, b_ref, tm, seq):
    i = pl.program_id(0)
    first = (i * tm) % seq == 0
    last = ((i + 1) * tm) % seq == 0
    pv = prev_ref[...].astype(F32)[prev_ref.shape[0] - 1:, :] * jnp.where(first, 0.0, 1.0)
    nx = next_ref[...].astype(F32)[0:1, :] * jnp.where(last, 0.0, 1.0)
    row = lax.broadcasted_iota(jnp.int32, x.shape, 0)
    xp = jnp.where(row == 0, pv, pltpu.roll(x, 1, 0))
    xn = jnp.where(row == tm - 1, nx, pltpu.roll(x, tm - 1, 0))
    return w_ref[0:1, :] * xp + w_ref[1:2, :] * x + w_ref[2:3, :] * xn + b_ref[...]


HALO = 16


def _halo_specs(tm, width, col, t):
    r = tm // HALO
    nblk = t // HALO
    return [
        pl.BlockSpec((tm, width), lambda i: (i, col)),
        pl.BlockSpec((HALO, width), lambda i: (jnp.maximum(i * r - 1, 0), col)),
        pl.BlockSpec((HALO, width), lambda i: (jnp.minimum((i + 1) * r, nblk - 1), col)),
    ]


def _ssd_conv_kernel(x_ref, prev_ref, next_ref, w_ref, b_ref, o_ref, *, tm, seq):
    o_ref[...] = _silu(_conv3(x_ref[...].astype(F32), prev_ref, next_ref, w_ref, b_ref, tm, seq))


def _ssd_conv_call(proj, w, b, seq, tm):
    t = proj.shape[0]
    return pl.pallas_call(
        functools.partial(_ssd_conv_kernel, tm=tm, seq=seq),
        grid=(t // tm,),
        in_specs=_halo_specs(tm, CONV_CH, C_XBC // CONV_CH, t) + [
            pl.BlockSpec((3, CONV_CH), lambda i: (0, 0)),
            pl.BlockSpec((1, CONV_CH), lambda i: (0, 0)),
        ],
        out_specs=pl.BlockSpec((tm, CONV_CH), lambda i: (i, 0)),
        out_shape=jax.ShapeDtypeStruct((t, CONV_CH), F32),
        compiler_params=_cparams("parallel"),
        name="ssd_conv",
    )(proj, proj, proj, w, b)


def _cumsum_rows(x, reverse):
    n = x.shape[0]
    row = lax.broadcasted_iota(jnp.int32, x.shape, 0)
    s = 1
    while s < n:
        if reverse:
            x = x + jnp.where(row < n - s, pltpu.roll(x, n - s, 0), 0.0)
        else:
            x = x + jnp.where(row >= s, pltpu.roll(x, s, 0), 0.0)
        s *= 2
    return x


def _ssd_kernel(*refs, nc, has_h0, nb):
    if has_h0:
        xs_ref, bs_ref, cs_ref, dt_ref, dtb_ref, alog_ref, dsk_ref, h0_ref = refs[:8]
        y_ref, hfin_ref, h_scr, hb_scr = refs[8:]
    else:
        xs_ref, bs_ref, cs_ref, dt_ref, dtb_ref, alog_ref, dsk_ref = refs[:7]
        y_ref, hfin_ref, h_scr, hb_scr = refs[7:]
    ph = pl.program_id(1)
    k = pl.program_id(2)
    q = CHUNK
    hpg = H_B // G_B
    gw = hpg * P_B
    lane = lax.broadcasted_iota(jnp.int32, (q, LANE), 1)
    row = lax.broadcasted_iota(jnp.int32, (q, LANE), 0)
    nt_dims = (((1,), (1,)), ((), ()))

    def decays(e):
        dtp = jax.nn.softplus(dt_ref[e] + dtb_ref[...])
        return dtp, dtp * (-jnp.exp(alog_ref[...]))

    def init_state(e, d):
        if has_h0:
            h_scr[e] = h0_ref[e, d]
        else:
            h_scr[e] = jnp.zeros((D_INNER, N_STATE), F32)

    def update_state(e, w, tot, off):
        w_t = w.T
        for g in range(G_B):
            bg = bs_ref[e, :, g * N_STATE:(g + 1) * N_STATE].astype(BF16)
            for pr in range(hpg // 2):
                hi = g * hpg + 2 * pr
                r0 = hi * P_B
                x_t = xs_ref[e, :, (hi // 2) * LANE:(hi // 2 + 1) * LANE].T
                w_rows = jnp.where(row < P_B, w_t[off + hi:off + hi + 1, :], w_t[off + hi + 1:off + hi + 2, :])
                st = jnp.dot((x_t * w_rows).astype(BF16), bg, preferred_element_type=F32)
                dec = jnp.where(row < P_B, jnp.exp(tot[:, off + hi:off + hi + 1]),
                                jnp.exp(tot[:, off + hi + 1:off + hi + 2]))
                h_scr[e, r0:r0 + 2 * P_B, :] = h_scr[e, r0:r0 + 2 * P_B, :] * dec + st

    def backward_chunk(e, c):
        dtp, la = decays(e)
        suf = _cumsum_rows(la, True)
        hb_scr[e, c] = h_scr[e]
        tot = suf[0:1, :]
        update_state(e, jnp.exp(tot - suf) * dtp, tot, H_B)

    def forward_chunk(e):
        dtp, la = decays(e)
        pre = _cumsum_rows(la, False)
        suf = _cumsum_rows(la, True)
        pre_t = pre.T
        suf_t = suf.T
        dt_t = dtp.T
        lower = row >= lane
        upper = row <= lane
        hb = hb_scr[e, k]
        for g in range(G_B):
            cg = cs_ref[e, :, g * N_STATE:(g + 1) * N_STATE].astype(BF16)
            bg = bs_ref[e, :, g * N_STATE:(g + 1) * N_STATE].astype(BF16)
            cb = lax.dot_general(cg, bg, nt_dims, preferred_element_type=F32)
            hf_g = h_scr[e, g * gw:(g + 1) * gw, :].astype(BF16)
            hb_g = hb[g * gw:(g + 1) * gw, :].astype(BF16)
            y_f = lax.dot_general(cg, hf_g, nt_dims, preferred_element_type=F32)
            y_b = lax.dot_general(cg, hb_g, nt_dims, preferred_element_type=F32)
            for pr in range(hpg // 2):
                hi0 = g * hpg + 2 * pr
                cols = slice((hi0 // 2) * LANE, (hi0 // 2 + 1) * LANE)
                xp = xs_ref[e, :, cols]
                xpb = xp.astype(BF16)
                acc = None
                e_f, e_b = [], []
                for half in range(2):
                    hi = hi0 + half
                    hb_i = H_B + hi
                    col_p = jnp.broadcast_to(pre[:, hi:hi + 1], (q, LANE))
                    col_s = jnp.broadcast_to(suf[:, hb_i:hb_i + 1], (q, LANE))
                    lf = jnp.exp(jnp.where(lower, col_p - pre_t[hi:hi + 1, :], -jnp.inf))
                    lb = jnp.exp(jnp.where(upper, col_s - suf_t[hb_i:hb_i + 1, :], -jnp.inf))
                    mm = cb * (lf * dt_t[hi:hi + 1, :] + lb * dt_t[hb_i:hb_i + 1, :])
                    keep = (lane < P_B) if half == 0 else (lane >= P_B)
                    xm = jnp.where(keep, xpb, jnp.zeros_like(xpb))
                    d = jnp.dot(mm.astype(BF16), xm, preferred_element_type=F32)
                    acc = d if acc is None else acc + d
                    e_f.append(jnp.exp(col_p))
                    e_b.append(jnp.exp(col_s))
                lo = pr * LANE
                y = (acc
                     + y_f[:, lo:lo + LANE] * jnp.where(lane < P_B, e_f[0], e_f[1])
                     + y_b[:, lo:lo + LANE] * jnp.where(lane < P_B, e_b[0], e_b[1])
                     + dsk_ref[:, cols] * xp)
                y_ref[e, :, cols] = y
        tot = pre[q - 1:q, :]
        update_state(e, jnp.exp(tot - pre) * dtp, tot, 0)

    @pl.when(ph == 0)
    def _():
        @pl.when(k == 0)
        def _():
            for e in range(nb):
                init_state(e, 1)

        for e in range(nb):
            backward_chunk(e, nc - 1 - k)

        @pl.when(k == nc - 1)
        def _():
            for e in range(nb):
                hfin_ref[e, 1] = h_scr[e]

    @pl.when(ph == 1)
    def _():
        @pl.when(k == 0)
        def _():
            for e in range(nb):
                init_state(e, 0)

        for e in range(nb):
            forward_chunk(e)

        @pl.when(k == nc - 1)
        def _():
            for e in range(nb):
                hfin_ref[e, 0] = h_scr[e]


SSD_BATCH = 2


def _ssd_call(xbc, dt_src, dt_col, dtb, alog, dsk, h0):
    b, seq = xbc.shape[0], xbc.shape[1]
    nc = seq // CHUNK
    has_h0 = h0 is not None
    nb = SSD_BATCH

    def cidx(ph, k):
        return jnp.where(ph == 0, nc - 1 - k, k)

    in_specs = [
        pl.BlockSpec((nb, CHUNK, D_INNER), lambda bi, ph, k: (bi, cidx(ph, k), 0)),
        pl.BlockSpec((nb, CHUNK, G_B * N_STATE), lambda bi, ph, k: (bi, cidx(ph, k), D_INNER // (G_B * N_STATE))),
        pl.BlockSpec((nb, CHUNK, G_B * N_STATE), lambda bi, ph, k: (bi, cidx(ph, k), D_INNER // (G_B * N_STATE) + 1)),
        pl.BlockSpec((nb, CHUNK, LANE), lambda bi, ph, k: (bi, cidx(ph, k), dt_col)),
        pl.BlockSpec((1, LANE), lambda bi, ph, k: (0, 0)),
        pl.BlockSpec((1, LANE), lambda bi, ph, k: (0, 0)),
        pl.BlockSpec((1, D_INNER), lambda bi, ph, k: (0, 0)),
    ]
    args = [xbc, xbc, xbc, dt_src, dtb, alog, dsk]
    if has_h0:
        in_specs.append(pl.BlockSpec((nb, 2, D_INNER, N_STATE), lambda bi, ph, k: (bi, 0, 0, 0)))
        args.append(h0)
    return pl.pallas_call(
        functools.partial(_ssd_kernel, nc=nc, has_h0=has_h0, nb=nb),
        grid=(b // nb, 2, nc),
        in_specs=in_specs,
        out_specs=[
            pl.BlockSpec((nb, CHUNK, D_INNER), lambda bi, ph, k: (bi, ph * k, 0)),
            pl.BlockSpec((nb, 2, D_INNER, N_STATE), lambda bi, ph, k: (bi, 0, 0, 0)),
        ],
        out_shape=[
            jax.ShapeDtypeStruct((b, seq, D_INNER), F32),
            jax.ShapeDtypeStruct((b, 2, D_INNER, N_STATE), F32),
        ],
        scratch_shapes=[
            pltpu.VMEM((nb, D_INNER, N_STATE), F32),
            pltpu.VMEM((nb, nc, D_INNER, N_STATE), F32),
        ],
        compiler_params=_cparams("parallel", "arbitrary", "arbitrary"),
        name="ssd",
    )(*args)


def _post_kernel(oa_ref, y_ref, z_ref, oc_ref, gate_ref, x_ref, mod_ref, gb_ref, g2_ref,
                 wa_ref, wb_ref, wc_ref, wo_ref, xo_ref, h2_ref, *, n_sub):
    sub = x_ref.shape[0] // n_sub
    for r in range(n_sub):
        rows = slice(r * sub, (r + 1) * sub)
        yb = _rms(y_ref[rows, :] * _silu(z_ref[rows, :].astype(F32)), gb_ref[...]).astype(BF16)
        out_a = jnp.dot(oa_ref[rows, :], wa_ref[...], preferred_element_type=F32)
        out_b = jnp.dot(yb, wb_ref[...], preferred_element_type=F32)
        out_c = jnp.dot(oc_ref[rows, :], wc_ref[...], preferred_element_type=F32)
        merged = (jax.nn.sigmoid(gate_ref[rows, 0:D_MODEL].astype(F32)) * out_a
                  + jax.nn.sigmoid(gate_ref[rows, D_MODEL:2 * D_MODEL].astype(F32)) * out_b
                  + jax.nn.sigmoid(gate_ref[rows, 2 * D_MODEL:3 * D_MODEL].astype(F32)) * out_c)
        out = jnp.dot(merged.astype(BF16), wo_ref[...], preferred_element_type=F32)
        xn = x_ref[rows, :] + mod_ref[0, 2:3, :] * out
        xo_ref[rows, :] = xn
        h2 = _rms(xn, g2_ref[...]) * (1.0 + mod_ref[0, 4:5, :]) + mod_ref[0, 3:4, :]
        h2_ref[rows, :] = h2.astype(BF16)


POST_SUB = 2


def _post_call(oa, y, proj, oc, x2d, mod, gb, g2, wa, wb, wc, wo, row_fn, tm):
    t = x2d.shape[0]
    full = lambda i: (0, 0)
    return pl.pallas_call(
        functools.partial(_post_kernel, n_sub=POST_SUB),
        grid=(t // tm,),
        in_specs=[
            pl.BlockSpec((tm, H_A * V_A), lambda i: (i, 0)),
            pl.BlockSpec((tm, D_INNER), lambda i: (i, 0)),
            pl.BlockSpec((tm, D_INNER), lambda i: (i, C_Z // D_INNER)),
            pl.BlockSpec((tm, H_C * DV_C), lambda i: (i, 0)),
            pl.BlockSpec((tm, 3 * D_MODEL), lambda i: (i, 0)),
            pl.BlockSpec((tm, D_MODEL), lambda i: (i, 0)),
            pl.BlockSpec((1, N_MOD, D_MODEL), lambda i: (row_fn(i, tm), 0, 0)),
            pl.BlockSpec((1, D_INNER), full),
            pl.BlockSpec((1, D_MODEL), full),
            pl.BlockSpec((H_A * V_A, D_MODEL), full),
            pl.BlockSpec((D_INNER, D_MODEL), full),
            pl.BlockSpec((H_C * DV_C, D_MODEL), full),
            pl.BlockSpec((D_MODEL, D_MODEL), full),
        ],
        out_specs=[
            pl.BlockSpec((tm, D_MODEL), lambda i: (i, 0)),
            pl.BlockSpec((tm, D_MODEL), lambda i: (i, 0)),
        ],
        out_shape=[
            jax.ShapeDtypeStruct((t, D_MODEL), F32),
            jax.ShapeDtypeStruct((t, D_MODEL), BF16),
        ],
        compiler_params=_cparams("parallel"),
        name="post_mix",
    )(oa, y, proj, oc, proj, x2d, mod, gb, g2, wa, wb, wc, wo)


FFN_CHUNK = 256
FFN_DOWN_PARTS = 2


def _ffn_kernel(h_ref, prev_ref, next_ref, wu_ref, cw_ref, cb_ref, wd_ref, x_ref, mod_ref, o_ref, a_scr, *, tm, seq):
    i = pl.program_id(0)
    ext = tm + 2 * HALO
    if tm > seq:
        pos = (i * tm + lax.broadcasted_iota(jnp.int32, (tm, 1), 0)) % seq
        has_prev = pos != 0
        has_next = pos != seq - 1
        hext = jnp.concatenate([prev_ref[...], h_ref[...], next_ref[...]], axis=0)
    else:
        keep_prev = jnp.where((i * tm) % seq == 0, 0.0, 1.0).astype(BF16)
        keep_next = jnp.where(((i + 1) * tm) % seq == 0, 0.0, 1.0).astype(BF16)
        hext = jnp.concatenate([prev_ref[...] * keep_prev, h_ref[...], next_ref[...] * keep_next], axis=0)

    def conv(u, c0):
        up = pltpu.roll(u, 1, 0)[HALO:HALO + tm]
        un = pltpu.roll(u, ext - 1, 0)[HALO:HALO + tm]
        if tm > seq:
            up = jnp.where(has_prev, up, 0.0)
            un = jnp.where(has_next, un, 0.0)
        cols = slice(c0, c0 + FFN_CHUNK)
        return (cw_ref[0:1, cols] * up + cw_ref[1:2, cols] * u[HALO:HALO + tm]
                + cw_ref[2:3, cols] * un + cb_ref[:, cols])

    n_chunks = D_FF // FFN_CHUNK
    ends = [(n_chunks * (p + 1)) // FFN_DOWN_PARTS * FFN_CHUNK for p in range(FFN_DOWN_PARTS)]
    out = None
    start = 0
    for c in range(n_chunks):
        g0, v0 = c * FFN_CHUNK, D_FF + c * FFN_CHUNK
        ug = jnp.dot(hext, wu_ref[:, g0:g0 + FFN_CHUNK], preferred_element_type=F32)
        uv = jnp.dot(hext, wu_ref[:, v0:v0 + FFN_CHUNK], preferred_element_type=F32)
        a_scr[:, g0:g0 + FFN_CHUNK] = (_silu(conv(ug, g0)) * conv(uv, v0)).astype(BF16)
        end = g0 + FFN_CHUNK
        if end in ends:
            part = jnp.dot(a_scr[:, start:end], wd_ref[start:end, :], preferred_element_type=F32)
            out = part if out is None else out + part
            start = end
    o_ref[...] = x_ref[...] + mod_ref[0, 5:6, :] * out


def _ffn_call(h2, wu, cw, cb, wd, x2d, mod, row_fn, seq, tm):
    t = h2.shape[0]
    n = 2 * D_FF
    full = lambda i: (0, 0)
    return pl.pallas_call(
        functools.partial(_ffn_kernel, tm=tm, seq=seq),
        grid=(t // tm,),
        in_specs=_halo_specs(tm, D_MODEL, 0, t) + [
            pl.BlockSpec((D_MODEL, n), full),
            pl.BlockSpec((3, n), full),
            pl.BlockSpec((1, n), full),
            pl.BlockSpec((D_FF, D_MODEL), full),
            pl.BlockSpec((tm, D_MODEL), lambda i: (i, 0)),
            pl.BlockSpec((1, N_MOD, D_MODEL), lambda i: (row_fn(i, tm), 0, 0)),
        ],
        out_specs=pl.BlockSpec((tm, D_MODEL), lambda i: (i, 0)),
        out_shape=jax.ShapeDtypeStruct((t, D_MODEL), F32),
        scratch_shapes=[pltpu.VMEM((tm, D_FF), BF16)],
        compiler_params=_cparams("parallel"),
        name="ffn",
    )(h2, h2, h2, wu, cw, cb, wd, x2d, mod)


def _final_norm_kernel(x_ref, g_ref, o_ref):
    o_ref[...] = _rms(x_ref[...], g_ref[...])


def _final_norm_call(x2d, g, tm):
    t = x2d.shape[0]
    return pl.pallas_call(
        _final_norm_kernel,
        grid=(t // tm,),
        in_specs=[pl.BlockSpec((tm, D_MODEL), lambda i: (i, 0)), pl.BlockSpec((1, D_MODEL), lambda i: (0, 0))],
        out_specs=pl.BlockSpec((tm, D_MODEL), lambda i: (i, 0)),
        out_shape=jax.ShapeDtypeStruct((t, D_MODEL), F32),
        compiler_params=_cparams("parallel"),
        name="final_norm",
    )(x2d, g)


def _rope_tables(n_tokens, rot_dim, lead, tail):
    rows = n_tokens // GRID_W
    row = jnp.broadcast_to(jnp.arange(rows, dtype=F32)[:, None], (rows, GRID_W)).reshape(-1)
    col = jnp.broadcast_to(jnp.arange(GRID_W, dtype=F32)[None, :], (rows, GRID_W)).reshape(-1)
    n_freq = rot_dim // 4
    freqs = ROPE_THETA ** (-jnp.arange(n_freq, dtype=F32) / n_freq)
    ang = jnp.concatenate([row[:, None] * freqs, col[:, None] * freqs], axis=-1)
    cos, sin = jnp.cos(ang), jnp.sin(ang)
    c = jnp.repeat(cos, 2, axis=-1)
    s = jnp.stack([-sin, sin], axis=-1).reshape(n_tokens, rot_dim)
    reps = (LANE - lead - tail) // rot_dim
    c = jnp.tile(c, (1, reps))
    s = jnp.tile(s, (1, reps))
    c = jnp.concatenate([jnp.ones((n_tokens, lead), F32), c, jnp.ones((n_tokens, tail), F32)], axis=-1)
    s = jnp.concatenate([jnp.zeros((n_tokens, lead), F32), s, jnp.zeros((n_tokens, tail), F32)], axis=-1)
    return c, s


def _pad_w_in(w):
    cq, ckv, krope, z, xbc, dt, qc, kc, vc, gate = jnp.split(
        w, [384, 640, 672, 1696, 3744, 3776, 4288, 4800, 5312], axis=-1)
    d = w.shape[0]
    zeros = lambda n: jnp.zeros((d, n), w.dtype)
    kr_pad = jnp.concatenate([zeros(NOPE_A), krope, zeros(LANE - NOPE_A - ROPE_A)], axis=-1)
    dt_pad = jnp.concatenate([dt, zeros(LANE - 2 * H_B)], axis=-1)
    main = jnp.concatenate([gate, z, xbc, qc, kc, vc, cq, kr_pad, ckv], axis=-1)
    side_ctx = jnp.concatenate([kc, vc, ckv, kr_pad, dt_pad], axis=-1)
    return main.astype(BF16), side_ctx.astype(BF16), dt_pad.astype(BF16)


S_KC, S_VC, S_CKV, S_KROPE, S_DT = 0, 512, 1024, 1280, 1408


def _prep_layer(l, w):
    d_q = NOPE_A + ROPE_A
    w_uq = w['w_uq'][l].reshape(Q_RANK, H_A, d_q)
    w_uq = jnp.pad(w_uq, ((0, 0), (0, 0), (0, LANE - d_q))).reshape(Q_RANK, H_A * LANE)
    w_ukv = w['w_ukv'][l].reshape(KV_RANK, H_A, NOPE_A + V_A)
    wk = jnp.pad(w_ukv[:, :, :NOPE_A], ((0, 0), (0, 0), (0, LANE - NOPE_A))).reshape(KV_RANK, H_A * LANE)
    wv = w_ukv[:, :, NOPE_A:].reshape(KV_RANK, H_A // 2, 2, V_A)
    zv = jnp.zeros_like(wv[:, :, 0])
    wv = jnp.stack([jnp.concatenate([wv[:, :, 0], zv], -1), jnp.concatenate([zv, wv[:, :, 1]], -1)],
                   axis=2).reshape(KV_RANK, H_A * LANE)
    pad_lane = lambda v: jnp.pad(v.reshape(1, -1), ((0, 0), (0, LANE - v.size)))
    w_in, w_side_ctx, w_side_lat = _pad_w_in(w['w_in'][l])
    return dict(
        w_in=w_in, w_side_ctx=w_side_ctx, w_side_lat=w_side_lat,
        norm1_g=w['norm1_g'][l].reshape(1, -1),
        norm2_g=w['norm2_g'][l].reshape(1, -1),
        q_norm_g=w['q_norm_g'][l].reshape(1, -1),
        kv_norm_g=w['kv_norm_g'][l].reshape(1, -1),
        w_uq=w_uq.astype(BF16),
        w_ukv=jnp.concatenate([wk, wv], axis=-1).astype(BF16),
        w_oA=w['w_oA'][l].astype(BF16),
        w_oB=w['w_oB'][l].astype(BF16),
        w_oC=w['w_oC'][l].astype(BF16),
        w_out=w['w_out'][l].astype(BF16),
        ssd_conv_w=w['ssd_conv_w'][l],
        ssd_conv_b=w['ssd_conv_b'][l].reshape(1, -1),
        dt_bias=pad_lane(w['ssd_dt_bias'][l]),
        a_log=pad_lane(w['ssd_a_log'][l]),
        d_skip=jnp.repeat(w['ssd_d_skip'][l], P_B).reshape(1, -1),
        ssd_norm_g=w['ssd_norm_g'][l].reshape(1, -1),
        diff_lambda=w['diff_lambda'][l],
        diff_norm_g=w['diff_norm_g'][l].reshape(1, -1),
        ffn_w_up=w['ffn_w_up'][l].astype(BF16),
        ffn_conv_w=w['ffn_conv_w'][l],
        ffn_conv_b=w['ffn_conv_b'][l].reshape(1, -1),
        ffn_w_down=w['ffn_w_down'][l].astype(BF16),
    )


def _tile(n, pref):
    t = min(n, pref)
    while n % t:
        t //= 2
    return t


def _block(x, mod, lp, lam_init, row_fn, tabs, ctx):
    b, seq, _ = x.shape
    t = b * seq
    x2d = x.reshape(t, D_MODEL)
    is_ctx = ctx is None
    tm_big = _tile(t if is_ctx else seq, 1024)
    tm = _tile(seq, 256)
    proj, side = _inproj_call(x2d, lp['norm1_g'], mod, lp['w_in'],
                              lp['w_side_ctx'] if is_ctx else lp['w_side_lat'], row_fn, tm_big)
    proj3 = proj.reshape(b, seq, N_PROJ)
    tab_a, tab_c = tabs if tabs is not None else (None, None)
    qa = _mla_q_call(proj, lp['q_norm_g'], lp['w_uq'], tab_a, seq, tm_big)
    if is_ctx:
        k_new, v_new, ckv_n = _kv_up_call(side, S_CKV // KV_RANK, side, S_KROPE // LANE, lp['kv_norm_g'],
                                          lp['w_ukv'], None, seq, tm_big, want_ckv=True)
    else:
        k_new, v_new = _kv_up_call(proj, C_CKV // KV_RANK, proj, C_KROPE // LANE, lp['kv_norm_g'], lp['w_ukv'],
                                   tab_a, seq, tm_big)
    wide = H_A * LANE
    pieces_a = [(k_new.reshape(b, seq, wide), 0, v_new.reshape(b, seq, wide), 0)]
    qd, kd = _dprep_call(proj, tab_c, seq, tm_big)
    pieces_c = [(kd.reshape(b, seq, H_C * LANE), 0, proj3, C_VC // LANE)]
    h0 = None
    if not is_ctx:
        ctx_ckv, ctx_krope, ctx_k, ctx_v, h0 = ctx
        past = ctx_ckv.shape[1]
        kr_pad = jnp.pad(ctx_krope.reshape(b * past, ROPE_A), ((0, 0), (NOPE_A, LANE - NOPE_A - ROPE_A)))
        k_c, v_c = _kv_up_call(ctx_ckv.reshape(b * past, KV_RANK), 0, kr_pad, 0, None, lp['w_ukv'],
                               None, past, _tile(b * past, 1024))
        pieces_a = [(k_c.reshape(b, past, wide), 0, v_c.reshape(b, past, wide), 0)] + pieces_a
        pieces_c = [(ctx_k.reshape(b, past, H_C * LANE), 0, ctx_v.reshape(b, past, H_C * DV_C), 0)] + pieces_c
        h0 = h0.reshape(b, 2, D_INNER, N_STATE)
    oa = _attn_call(qa.reshape(b, seq, wide), pieces_a, H_A, False, _tile(seq, 1024), H_A if is_ctx else 2)
    oc = _attn_call(qd.reshape(b, seq, H_C * LANE), pieces_c, H_C, True, _tile(seq, 512), H_C,
                    extra=(lp['diff_lambda'], lp['diff_norm_g']), lam_init=lam_init)
    xbc = _ssd_conv_call(proj, lp['ssd_conv_w'], lp['ssd_conv_b'], seq, tm)
    y, hfin = _ssd_call(xbc.reshape(b, seq, CONV_CH), side.reshape(b, seq, side.shape[1]),
                        S_DT // LANE if is_ctx else 0, lp['dt_bias'], lp['a_log'], lp['d_skip'], h0)
    x2d, h2 = _post_call(oa.reshape(t, H_A * V_A), y.reshape(t, D_INNER), proj, oc.reshape(t, H_C * DV_C), x2d, mod,
                         lp['ssd_norm_g'], lp['norm2_g'], lp['w_oA'], lp['w_oB'], lp['w_oC'], lp['w_out'],
                         row_fn, _tile(t if is_ctx else seq, 512))
    x2d = _ffn_call(h2, lp['ffn_w_up'], lp['ffn_conv_w'], lp['ffn_conv_b'], lp['ffn_w_down'], x2d, mod, row_fn,
                    seq, _tile(t if is_ctx else seq, 1024))
    st = None
    if is_ctx:
        st = (ckv_n.reshape(b, seq, KV_RANK),
              side[:, S_KROPE + NOPE_A:S_KROPE + NOPE_A + ROPE_A].reshape(b, seq, ROPE_A),
              side[:, S_KC:S_KC + H_C * 2 * DH_C].reshape(b, seq, H_C, 2, DH_C),
              side[:, S_VC:S_VC + H_C * DV_C].reshape(b, seq, H_C, DV_C),
              hfin.reshape(b, 2, H_B, P_B, N_STATE))
    return x2d.reshape(b, seq, D_MODEL), st


def kernel(x_prompt, x_sample, cache_mla_ckv, cache_mla_krope, cache_diff_k, cache_diff_v, state_ssd, c, c_ctx, w_mod, b_mod, norm1_g, norm2_g, w_in, q_norm_g, kv_norm_g, w_uq, w_ukv, w_oA, ssd_conv_w, ssd_conv_b, ssd_dt_bias, ssd_a_log, ssd_d_skip, ssd_norm_g, w_oB, diff_lambda, diff_norm_g, w_oC, w_out, ffn_w_up, ffn_conv_w, ffn_conv_b, ffn_w_down, final_norm_g):
    weights = dict(norm1_g=norm1_g, norm2_g=norm2_g, w_in=w_in, q_norm_g=q_norm_g, kv_norm_g=kv_norm_g,
                   w_uq=w_uq, w_ukv=w_ukv, w_oA=w_oA, ssd_conv_w=ssd_conv_w, ssd_conv_b=ssd_conv_b,
                   ssd_dt_bias=ssd_dt_bias, ssd_a_log=ssd_a_log, ssd_d_skip=ssd_d_skip, ssd_norm_g=ssd_norm_g,
                   w_oB=w_oB, diff_lambda=diff_lambda, diff_norm_g=diff_norm_g, w_oC=w_oC, w_out=w_out,
                   ffn_w_up=ffn_w_up, ffn_conv_w=ffn_conv_w, ffn_conv_b=ffn_conv_b, ffn_w_down=ffn_w_down)
    n_dec, n_lat = x_sample.shape[0], x_sample.shape[1]
    ctx_row = n_dec
    cond = jnp.concatenate([c, c_ctx[None, :], jnp.zeros((COND_ROWS - n_dec - 1, D_MODEL), F32)], axis=0)
    mod_all = _mod_call(cond, w_mod, b_mod).reshape(DEPTH, COND_ROWS, N_MOD, D_MODEL)
    tabs = (_rope_tables(n_lat, ROPE_A, NOPE_A, LANE - NOPE_A - ROPE_A), _rope_tables(n_lat, DH_C, 0, 0))
    lat_row = lambda i, tm: (i * tm) // n_lat
    ctx_row_fn = lambda i, tm: ctx_row
    xp, xs = x_prompt, x_sample
    new = [[] for _ in range(5)]
    for l in range(DEPTH):
        lp = _prep_layer(l, weights)
        lam_init = 0.8 - 0.6 * math.exp(-0.3 * l)
        xp, st = _block(xp, mod_all[l], lp, lam_init, ctx_row_fn, None, None)
        for acc, v in zip(new, st):
            acc.append(v)
        ctx = (cache_mla_ckv[:, l], cache_mla_krope[:, l], cache_diff_k[:, l], cache_diff_v[:, l], state_ssd[:, l])
        xs, _ = _block(xs, mod_all[l], lp, lam_init, lat_row, tabs, ctx)
    bp, sp = xp.shape[0], xp.shape[1]
    y_prompt = _final_norm_call(xp.reshape(bp * sp, D_MODEL), final_norm_g.reshape(1, -1), _tile(bp * sp, 1024))
    y_sample = _final_norm_call(xs.reshape(n_dec * n_lat, D_MODEL), final_norm_g.reshape(1, -1),
                                _tile(n_dec * n_lat, 1024))
    return (y_prompt.reshape(bp, sp, D_MODEL), y_sample.reshape(n_dec, n_lat, D_MODEL),
            jnp.stack(new[0], axis=1), jnp.stack(new[1], axis=1), jnp.stack(new[2], axis=1),
            jnp.stack(new[3], axis=1), jnp.stack(new[4], axis=1))
```

```python
import functools
import math

import jax
import jax.numpy as jnp
from jax import lax
from jax.experimental import pallas as pl
from jax.experimental.pallas import tpu as pltpu

F32 = jnp.float32
BF16 = jnp.bfloat16

D_MODEL = 1024
DEPTH = 4
GRID_W = 64
ROPE_THETA = 10000.0
EPS = 1e-6
H_A, NOPE_A, ROPE_A, V_A = 8, 64, 32, 64
Q_RANK, KV_RANK = 384, 256
H_B, P_B, N_STATE, G_B, CHUNK = 16, 64, 128, 4, 128
D_INNER = H_B * P_B
CONV_CH = D_INNER + 2 * G_B * N_STATE
H_C, DH_C = 4, 64
DV_C = 2 * DH_C
D_FF = 2816
N_MOD = 6
COND_ROWS = 16

LOG2E = math.log2(math.e)
LANE = 128
SUBLANE = 8
VMEM_LIMIT = 56 * 1024 * 1024

C_GATE = 0
C_Z = 3072
C_XBC = 4096
C_QC = 6144
C_KC = 6656
C_VC = 7168
C_CQ = 7680
C_KROPE = 8064
C_CKV = 8192
N_PROJ = 8448
PROJ_TILE = 2816


def _cparams(*sem):
    return pltpu.CompilerParams(dimension_semantics=sem, vmem_limit_bytes=VMEM_LIMIT)


def _silu(x):
    return x * jax.nn.sigmoid(x)


def _rms(x, g):
    ms = jnp.mean(x * x, axis=-1, keepdims=True)
    return x * lax.rsqrt(ms + EPS) * g


def _rope(x, c, s):
    n = x.shape[-1]
    lane = lax.broadcasted_iota(jnp.int32, x.shape, x.ndim - 1)
    nxt = pltpu.roll(x, n - 1, x.ndim - 1)
    prv = pltpu.roll(x, 1, x.ndim - 1)
    return x * c + jnp.where(lane % 2 == 0, nxt, prv) * s


def _mod_kernel(c_ref, w_ref, b_ref, o_ref):
    s = _silu(c_ref[...]).astype(BF16)
    o_ref[0] = jnp.dot(s, w_ref[0].astype(BF16), preferred_element_type=F32) + b_ref[0]


def _mod_call(cond, w_mod, b_mod):
    tn = 1536
    n = N_MOD * D_MODEL
    return pl.pallas_call(
        _mod_kernel,
        grid=(DEPTH, n // tn),
        in_specs=[
            pl.BlockSpec((COND_ROWS, D_MODEL), lambda l, j: (0, 0)),
            pl.BlockSpec((1, D_MODEL, tn), lambda l, j: (l, 0, j)),
            pl.BlockSpec((1, 1, tn), lambda l, j: (l, 0, j)),
        ],
        out_specs=pl.BlockSpec((1, COND_ROWS, tn), lambda l, j: (l, 0, j)),
        out_shape=jax.ShapeDtypeStruct((DEPTH, COND_ROWS, n), F32),
        compiler_params=_cparams("parallel", "parallel"),
        name="mod",
    )(cond, w_mod, b_mod.reshape(DEPTH, 1, n))


INPROJ_SUB = 4


def _inproj_kernel(x_ref, g_ref, mod_ref, w_ref, ws_ref, o_ref, os_ref, h_scr):
    j = pl.program_id(1)

    @pl.when(j == 0)
    def _():
        sub = x_ref.shape[0] // INPROJ_SUB
        for r in range(INPROJ_SUB):
            rows = slice(r * sub, (r + 1) * sub)
            h = _rms(x_ref[rows, :], g_ref[...]) * (1.0 + mod_ref[0, 1:2, :]) + mod_ref[0, 0:1, :]
            hb = h.astype(BF16)
            h_scr[rows, :] = hb
            os_ref[rows, :] = jnp.dot(hb, ws_ref[...], preferred_element_type=F32)
            o_ref[rows, :] = jnp.dot(hb, w_ref[...], preferred_element_type=F32).astype(BF16)

    @pl.when(j > 0)
    def _():
        o_ref[...] = jnp.dot(h_scr[...], w_ref[...], preferred_element_type=F32).astype(BF16)


def _inproj_call(x2d, g, mod, w, w_side, row_fn, tm):
    t = x2d.shape[0]
    tn = PROJ_TILE
    n_side = w_side.shape[1]
    return pl.pallas_call(
        _inproj_kernel,
        grid=(t // tm, N_PROJ // tn),
        in_specs=[
            pl.BlockSpec((tm, D_MODEL), lambda i, j: (i, 0)),
            pl.BlockSpec((1, D_MODEL), lambda i, j: (0, 0)),
            pl.BlockSpec((1, N_MOD, D_MODEL), lambda i, j: (row_fn(i, tm), 0, 0)),
            pl.BlockSpec((D_MODEL, tn), lambda i, j: (0, j)),
            pl.BlockSpec((D_MODEL, n_side), lambda i, j: (0, 0)),
        ],
        out_specs=[
            pl.BlockSpec((tm, tn), lambda i, j: (i, j)),
            pl.BlockSpec((tm, n_side), lambda i, j: (i, 0)),
        ],
        out_shape=[
            jax.ShapeDtypeStruct((t, N_PROJ), BF16),
            jax.ShapeDtypeStruct((t, n_side), F32),
        ],
        scratch_shapes=[pltpu.VMEM((tm, D_MODEL), BF16)],
        compiler_params=_cparams("parallel", "arbitrary"),
        name="inproj",
    )(x2d, g, mod, w, w_side)


def _mla_q_kernel(*refs, rope, scale):
    if rope:
        p_ref, g_ref, w_ref, c_ref, s_ref, o_ref = refs
    else:
        p_ref, g_ref, w_ref, o_ref = refs
    cqn = _rms(p_ref[:, :Q_RANK].astype(F32), g_ref[...]).astype(BF16)
    q = jnp.dot(cqn, w_ref[...], preferred_element_type=F32)
    for h in range(H_A):
        qh = q[:, h * LANE:(h + 1) * LANE]
        if rope:
            qh = _rope(qh, c_ref[...], s_ref[...])
        o_ref[:, h * LANE:(h + 1) * LANE] = (qh * scale).astype(BF16)


def _mla_q_call(proj, g, w, tabs, seq, tm):
    t = proj.shape[0]
    rope = tabs is not None
    nb = seq // tm
    in_specs = [
        pl.BlockSpec((tm, 512), lambda i: (i, C_CQ // 512)),
        pl.BlockSpec((1, Q_RANK), lambda i: (0, 0)),
        pl.BlockSpec((Q_RANK, H_A * LANE), lambda i: (0, 0)),
    ]
    args = [proj, g, w]
    if rope:
        in_specs += [pl.BlockSpec((tm, LANE), lambda i: (i % nb, 0))] * 2
        args += list(tabs)
    return pl.pallas_call(
        functools.partial(_mla_q_kernel, rope=rope, scale=LOG2E * (NOPE_A + ROPE_A) ** -0.5),
        grid=(t // tm,),
        in_specs=in_specs,
        out_specs=pl.BlockSpec((tm, H_A * LANE), lambda i: (i, 0)),
        out_shape=jax.ShapeDtypeStruct((t, H_A * LANE), BF16),
        compiler_params=_cparams("parallel"),
        name="mla_q",
    )(*args)


def _kv_up_kernel(*refs, norm, rope, want_ckv):
    refs = list(refs)
    ckv_ref, kr_ref = refs[0], refs[1]
    pos = 2
    if norm:
        g_ref = refs[pos]
        pos += 1
    w_ref = refs[pos]
    pos += 1
    if rope:
        c_ref, s_ref = refs[pos], refs[pos + 1]
        pos += 2
    k_ref, v_ref = refs[pos], refs[pos + 1]
    ckv = ckv_ref[...].astype(F32)
    if norm:
        ckv = _rms(ckv, g_ref[...])
        if want_ckv:
            refs[pos + 2][...] = ckv
    kv = jnp.dot(ckv.astype(BF16), w_ref[...], preferred_element_type=F32)
    kr = kr_ref[...].astype(F32)
    if rope:
        kr = _rope(kr, c_ref[...], s_ref[...])
    for h in range(H_A):
        k_ref[:, h * LANE:(h + 1) * LANE] = (kv[:, h * LANE:(h + 1) * LANE] + kr).astype(BF16)
    v_ref[...] = kv[:, H_A * LANE:].astype(BF16)


def _kv_up_call(ckv_src, ckv_col, kr_src, kr_col, g, w, tabs, seq, tm, want_ckv=False):
    t = ckv_src.shape[0]
    norm = g is not None
    rope = tabs is not None
    nb = seq // tm
    in_specs = [
        pl.BlockSpec((tm, KV_RANK), lambda i: (i, ckv_col)),
        pl.BlockSpec((tm, LANE), lambda i: (i, kr_col)),
    ]
    args = [ckv_src, kr_src]
    if norm:
        in_specs.append(pl.BlockSpec((1, KV_RANK), lambda i: (0, 0)))
        args.append(g)
    in_specs.append(pl.BlockSpec((KV_RANK, 2 * H_A * LANE), lambda i: (0, 0)))
    args.append(w)
    if rope:
        in_specs += [pl.BlockSpec((tm, LANE), lambda i: (i % nb, 0))] * 2
        args += list(tabs)
    out_specs = [pl.BlockSpec((tm, H_A * LANE), lambda i: (i, 0))] * 2
    out_shape = [jax.ShapeDtypeStruct((t, H_A * LANE), BF16)] * 2
    if want_ckv:
        out_specs.append(pl.BlockSpec((tm, KV_RANK), lambda i: (i, 0)))
        out_shape.append(jax.ShapeDtypeStruct((t, KV_RANK), F32))
    return pl.pallas_call(
        functools.partial(_kv_up_kernel, norm=norm, rope=rope, want_ckv=want_ckv),
        grid=(t // tm,),
        in_specs=in_specs,
        out_specs=out_specs,
        out_shape=out_shape,
        compiler_params=_cparams("parallel"),
        name="kv_up",
    )(*args)


def _dprep_kernel(*refs, rope):
    if rope:
        q_ref, k_ref, c_ref, s_ref, qo_ref, ko_ref = refs
    else:
        q_ref, k_ref, qo_ref, ko_ref = refs
    for b in range(H_C):
        q = q_ref[:, b * LANE:(b + 1) * LANE].astype(F32)
        if rope:
            q = _rope(q, c_ref[...], s_ref[...])
        qo_ref[:, b * LANE:(b + 1) * LANE] = (q * (LOG2E * DH_C ** -0.5)).astype(BF16)
    for b in range(H_C):
        k = k_ref[:, b * LANE:(b + 1) * LANE].astype(F32)
        if rope:
            k = _rope(k, c_ref[...], s_ref[...])
        ko_ref[:, b * LANE:(b + 1) * LANE] = k.astype(BF16)


def _dprep_call(proj, tabs, seq, tm):
    t = proj.shape[0]
    rope = tabs is not None
    nb = seq // tm
    in_specs = [
        pl.BlockSpec((tm, 512), lambda i: (i, C_QC // 512)),
        pl.BlockSpec((tm, 512), lambda i: (i, C_KC // 512)),
    ]
    args = [proj, proj]
    if rope:
        in_specs += [pl.BlockSpec((tm, LANE), lambda i: (i % nb, 0))] * 2
        args += list(tabs)
    return pl.pallas_call(
        functools.partial(_dprep_kernel, rope=rope),
        grid=(t // tm,),
        in_specs=in_specs,
        out_specs=[
            pl.BlockSpec((tm, 512), lambda i: (i, 0)),
            pl.BlockSpec((tm, 512), lambda i: (i, 0)),
        ],
        out_shape=[
            jax.ShapeDtypeStruct((t, 512), BF16),
            jax.ShapeDtypeStruct((t, 512), BF16),
        ],
        compiler_params=_cparams("parallel"),
        name="dprep",
    )(*args)


def _attn_kernel(*refs, chunks, n_pieces, diff, tq, hp, lam_init):
    q_ref = refs[0]
    kv = refs[1:1 + 2 * n_pieces]
    o_ref = refs[-1]
    nt = (((1,), (1,)), ((), ()))

    def lane_sum(p):
        parts = [p[:, c * LANE:(c + 1) * LANE] for c in range(p.shape[1] // LANE)]
        while len(parts) > 1:
            parts = [a + b for a, b in zip(parts[::2], parts[1::2])] + ([parts[-1]] if len(parts) % 2 else [])
        return parts[0]

    def one_head(h):
        cols = slice(h * LANE, (h + 1) * LANE)
        q = q_ref[0, :, cols]
        if diff:
            lane = lax.broadcasted_iota(jnp.int32, q.shape, 1)
            zero = jnp.zeros_like(q)
            q = jnp.concatenate([jnp.where(lane < DH_C, q, zero), jnp.where(lane >= DH_C, q, zero)], axis=0)
        m = l = acc = None
        for pi, st, sz in chunks:
            k = kv[2 * pi][0, st:st + sz, cols].astype(BF16)
            v = kv[2 * pi + 1][0, st:st + sz, cols].astype(BF16)
            s = lax.dot_general(q, k, nt, preferred_element_type=F32)
            ms = s.max(axis=-1, keepdims=True)
            if m is None:
                m = ms
                p = jnp.exp2(s - m)
                l = lane_sum(p)
                acc = jnp.dot(p.astype(BF16), v, preferred_element_type=F32)
            else:
                m_new = jnp.maximum(m, ms)
                alpha = jnp.exp2(m - m_new)
                p = jnp.exp2(s - m_new)
                l = alpha * l + lane_sum(p)
                acc = alpha * acc + jnp.dot(p.astype(BF16), v, preferred_element_type=F32)
                m = m_new
        return acc / l.sum(axis=-1, keepdims=True)

    if diff:
        lam_ref, g_ref = refs[1 + 2 * n_pieces], refs[2 + 2 * n_pieces]
        lv = lam_ref[...]
        lam = (jnp.exp(jnp.sum(lv[0:1] * lv[1:2], axis=-1, keepdims=True))
               - jnp.exp(jnp.sum(lv[2:3] * lv[3:4], axis=-1, keepdims=True)) + lam_init)
        for h in range(hp):
            o = one_head(h)
            oc = o[:tq] - lam * o[tq:]
            o_ref[0, :, h * LANE:(h + 1) * LANE] = (_rms(oc, g_ref[...]) * (1.0 - lam_init)).astype(BF16)
    else:
        for h in range(0, hp, 2):
            o_ref[0, :, (h // 2) * LANE:(h // 2 + 1) * LANE] = (one_head(h) + one_head(h + 1)).astype(BF16)


KEY_CHUNK = 256


def _attn_call(q, pieces, n_heads, diff, tq, hp, extra=(), lam_init=0.0):
    b, seq = q.shape[0], q.shape[1]
    ow = hp * LANE if diff else hp * V_A
    in_specs = [pl.BlockSpec((1, tq, hp * LANE), lambda bi, h, qi: (bi, qi, h))]
    args = [q]
    chunks = []
    for pi, (k, kc, v, vc) in enumerate(pieces):
        s = k.shape[1]
        in_specs += [pl.BlockSpec((1, s, hp * LANE), lambda bi, h, qi, kc=kc: (bi, 0, kc // hp + h)),
                     pl.BlockSpec((1, s, hp * LANE), lambda bi, h, qi, vc=vc: (bi, 0, vc // hp + h))]
        args += [k, v]
        step = _tile(s, KEY_CHUNK)
        chunks += [(pi, st, step) for st in range(0, s, step)]
    for e in extra:
        in_specs.append(pl.BlockSpec(e.shape, lambda bi, h, qi: (0, 0)))
        args.append(e)
    return pl.pallas_call(
        functools.partial(_attn_kernel, chunks=tuple(chunks), n_pieces=len(pieces), diff=diff, tq=tq, hp=hp,
                          lam_init=lam_init),
        grid=(b, n_heads // hp, seq // tq),
        in_specs=in_specs,
        out_specs=pl.BlockSpec((1, tq, ow), lambda bi, h, qi: (bi, qi, h)),
        out_shape=jax.ShapeDtypeStruct((b, seq, (n_heads // hp) * ow), BF16),
        compiler_params=_cparams("parallel", "parallel", "arbitrary"),
        name="diff_attn" if diff else "mla_attn",
    )(*args)


def _conv3(x, prev_ref, next_ref, w_ref, b_ref, tm, seq):
    i = pl.program_id(0)
    first = (i * tm) % seq == 0
    last = ((i + 1) * tm) % seq == 0
    pv = prev_ref[...].astype(F32)[prev_ref.shape[0] - 1:, :] * jnp.where(first, 0.0, 1.0)
    nx = next_ref[...].astype(F32)[0:1, :] * jnp.where(last, 0.0, 1.0)
    row = lax.broadcasted_iota(jnp.int32, x.shape, 0)
    xp = jnp.where(row == 0, pv, pltpu.roll(x, 1, 0))
    xn = jnp.where(row == tm - 1, nx, pltpu.roll(x, tm - 1, 0))
    return w_ref[0:1, :] * xp + w_ref[1:2, :] * x + w_ref[2:3, :] * xn + b_ref[...]


HALO = 16


def _halo_specs(tm, width, col, t):
    r = tm // HALO
    nblk = t // HALO
    return [
        pl.BlockSpec((tm, width), lambda i: (i, col)),
        pl.BlockSpec((HALO, width), lambda i: (jnp.maximum(i * r - 1, 0), col)),
        pl.BlockSpec((HALO, width), lambda i: (jnp.minimum((i + 1) * r, nblk - 1), col)),
    ]


def _ssd_conv_kernel(x_ref, prev_ref, next_ref, w_ref, b_ref, o_ref, *, tm, seq):
    o_ref[...] = _silu(_conv3(x_ref[...].astype(F32), prev_ref, next_ref, w_ref, b_ref, tm, seq))


def _ssd_conv_call(proj, w, b, seq, tm):
    t = proj.shape[0]
    return pl.pallas_call(
        functools.partial(_ssd_conv_kernel, tm=tm, seq=seq),
        grid=(t // tm,),
        in_specs=_halo_specs(tm, CONV_CH, C_XBC // CONV_CH, t) + [
            pl.BlockSpec((3, CONV_CH), lambda i: (0, 0)),
            pl.BlockSpec((1, CONV_CH), lambda i: (0, 0)),
        ],
        out_specs=pl.BlockSpec((tm, CONV_CH), lambda i: (i, 0)),
        out_shape=jax.ShapeDtypeStruct((t, CONV_CH), F32),
        compiler_params=_cparams("parallel"),
        name="ssd_conv",
    )(proj, proj, proj, w, b)


def _cumsum_rows(x, reverse):
    n = x.shape[0]
    row = lax.broadcasted_iota(jnp.int32, x.shape, 0)
    s = 1
    while s < n:
        if reverse:
            x = x + jnp.where(row < n - s, pltpu.roll(x, n - s, 0), 0.0)
        else:
            x = x + jnp.where(row >= s, pltpu.roll(x, s, 0), 0.0)
        s *= 2
    return x


def _ssd_kernel(*refs, nc, has_h0, nb):
    if has_h0:
        xs_ref, bs_ref, cs_ref, dt_ref, dtb_ref, alog_ref, dsk_ref, h0_ref = refs[:8]
        y_ref, hfin_ref, h_scr, hb_scr, aux_scr = refs[8:]
    else:
        xs_ref, bs_ref, cs_ref, dt_ref, dtb_ref, alog_ref, dsk_ref = refs[:7]
        y_ref, hfin_ref, h_scr, hb_scr, aux_scr = refs[7:]
    ph = pl.program_id(1)
    k = pl.program_id(2)
    q = CHUNK
    hpg = H_B // G_B
    gw = hpg * P_B
    lane = lax.broadcasted_iota(jnp.int32, (q, LANE), 1)
    row = lax.broadcasted_iota(jnp.int32, (q, LANE), 0)
    nt_dims = (((1,), (1,)), ((), ()))

    def decays(e):
        dtp = jax.nn.softplus(dt_ref[e] + dtb_ref[...])
        return dtp, dtp * (-jnp.exp(alog_ref[...]))

    def init_state(e, d):
        if has_h0:
            h_scr[e] = h0_ref[e, d]
        else:
            h_scr[e] = jnp.zeros((D_INNER, N_STATE), F32)

    def update_state(e, w, tot, off):
        w_t = w.T
        for g in range(G_B):
            bg = bs_ref[e, :, g * N_STATE:(g + 1) * N_STATE].astype(BF16)
            for pr in range(hpg // 2):
                hi = g * hpg + 2 * pr
                r0 = hi * P_B
                x_t = xs_ref[e, :, (hi // 2) * LANE:(hi // 2 + 1) * LANE].T
                w_rows = jnp.where(row < P_B, w_t[off + hi:off + hi + 1, :], w_t[off + hi + 1:off + hi + 2, :])
                st = jnp.dot((x_t * w_rows).astype(BF16), bg, preferred_element_type=F32)
                dec = jnp.where(row < P_B, jnp.exp(tot[:, off + hi:off + hi + 1]),
                                jnp.exp(tot[:, off + hi + 1:off + hi + 2]))
                h_scr[e, r0:r0 + 2 * P_B, :] = h_scr[e, r0:r0 + 2 * P_B, :] * dec + st

    def backward_chunk(e, c):
        dtp, la = decays(e)
        suf = _cumsum_rows(la, True)
        aux_scr[e, c, 0] = dtp
        aux_scr[e, c, 1] = suf
        hb_scr[e, c] = h_scr[e]
        tot = suf[0:1, :]
        update_state(e, jnp.exp(tot - suf) * dtp, tot, H_B)

    def forward_chunk(e):
        dtp = aux_scr[e, k, 0]
        suf = aux_scr[e, k, 1]
        pre = _cumsum_rows(dtp * (-jnp.exp(alog_ref[...])), False)
        pre_t = pre.T
        suf_t = suf.T
        dt_t = dtp.T
        lower = row >= lane
        upper = row <= lane
        hb = hb_scr[e, k]
        for g in range(G_B):
            cg = cs_ref[e, :, g * N_STATE:(g + 1) * N_STATE].astype(BF16)
            bg = bs_ref[e, :, g * N_STATE:(g + 1) * N_STATE].astype(BF16)
            cb = lax.dot_general(cg, bg, nt_dims, preferred_element_type=F32)
            hf_g = h_scr[e, g * gw:(g + 1) * gw, :].astype(BF16)
            hb_g = hb[g * gw:(g + 1) * gw, :].astype(BF16)
            y_f = lax.dot_general(cg, hf_g, nt_dims, preferred_element_type=F32)
            y_b = lax.dot_general(cg, hb_g, nt_dims, preferred_element_type=F32)
            for pr in range(hpg // 2):
                hi0 = g * hpg + 2 * pr
                cols = slice((hi0 // 2) * LANE, (hi0 // 2 + 1) * LANE)
                xp = xs_ref[e, :, cols]
                xpb = xp.astype(BF16)
                acc = None
                e_f, e_b = [], []
                for half in range(2):
                    hi = hi0 + half
                    hb_i = H_B + hi
                    col_p = jnp.broadcast_to(pre[:, hi:hi + 1], (q, LANE))
                    col_s = jnp.broadcast_to(suf[:, hb_i:hb_i + 1], (q, LANE))
                    lf = jnp.exp(jnp.where(lower, col_p - pre_t[hi:hi + 1, :], -jnp.inf))
                    lb = jnp.exp(jnp.where(upper, col_s - suf_t[hb_i:hb_i + 1, :], -jnp.inf))
                    mm = cb * (lf * dt_t[hi:hi + 1, :] + lb * dt_t[hb_i:hb_i + 1, :])
                    keep = (lane < P_B) if half == 0 else (lane >= P_B)
                    xm = jnp.where(keep, xpb, jnp.zeros_like(xpb))
                    d = jnp.dot(mm.astype(BF16), xm, preferred_element_type=F32)
                    acc = d if acc is None else acc + d
                    e_f.append(jnp.exp(col_p))
                    e_b.append(jnp.exp(col_s))
                lo = pr * LANE
                y = (acc
                     + y_f[:, lo:lo + LANE] * jnp.where(lane < P_B, e_f[0], e_f[1])
                     + y_b[:, lo:lo + LANE] * jnp.where(lane < P_B, e_b[0], e_b[1])
                     + dsk_ref[:, cols] * xp)
                y_ref[e, :, cols] = y
        tot = pre[q - 1:q, :]
        update_state(e, jnp.exp(tot - pre) * dtp, tot, 0)

    @pl.when(ph == 0)
    def _():
        @pl.when(k == 0)
        def _():
            for e in range(nb):
                init_state(e, 1)

        for e in range(nb):
            backward_chunk(e, nc - 1 - k)

        @pl.when(k == nc - 1)
        def _():
            for e in range(nb):
                hfin_ref[e, 1] = h_scr[e]

    @pl.when(ph == 1)
    def _():
        @pl.when(k == 0)
        def _():
            for e in range(nb):
                init_state(e, 0)

        for e in range(nb):
            forward_chunk(e)

        @pl.when(k == nc - 1)
        def _():
            for e in range(nb):
                hfin_ref[e, 0] = h_scr[e]


SSD_BATCH = 2


def _ssd_call(xbc, dt_src, dt_col, dtb, alog, dsk, h0):
    b, seq = xbc.shape[0], xbc.shape[1]
    nc = seq // CHUNK
    has_h0 = h0 is not None
    nb = SSD_BATCH

    def cidx(ph, k):
        return jnp.where(ph == 0, nc - 1 - k, k)

    in_specs = [
        pl.BlockSpec((nb, CHUNK, D_INNER), lambda bi, ph, k: (bi, cidx(ph, k), 0)),
        pl.BlockSpec((nb, CHUNK, G_B * N_STATE), lambda bi, ph, k: (bi, cidx(ph, k), D_INNER // (G_B * N_STATE))),
        pl.BlockSpec((nb, CHUNK, G_B * N_STATE), lambda bi, ph, k: (bi, cidx(ph, k), D_INNER // (G_B * N_STATE) + 1)),
        pl.BlockSpec((nb, CHUNK, LANE), lambda bi, ph, k: (bi, cidx(ph, k), dt_col)),
        pl.BlockSpec((1, LANE), lambda bi, ph, k: (0, 0)),
        pl.BlockSpec((1, LANE), lambda bi, ph, k: (0, 0)),
        pl.BlockSpec((1, D_INNER), lambda bi, ph, k: (0, 0)),
    ]
    args = [xbc, xbc, xbc, dt_src, dtb, alog, dsk]
    if has_h0:
        in_specs.append(pl.BlockSpec((nb, 2, D_INNER, N_STATE), lambda bi, ph, k: (bi, 0, 0, 0)))
        args.append(h0)
    return pl.pallas_call(
        functools.partial(_ssd_kernel, nc=nc, has_h0=has_h0, nb=nb),
        grid=(b // nb, 2, nc),
        in_specs=in_specs,
        out_specs=[
            pl.BlockSpec((nb, CHUNK, D_INNER), lambda bi, ph, k: (bi, ph * k, 0)),
            pl.BlockSpec((nb, 2, D_INNER, N_STATE), lambda bi, ph, k: (bi, 0, 0, 0)),
        ],
        out_shape=[
            jax.ShapeDtypeStruct((b, seq, D_INNER), F32),
            jax.ShapeDtypeStruct((b, 2, D_INNER, N_STATE), F32),
        ],
        scratch_shapes=[
            pltpu.VMEM((nb, D_INNER, N_STATE), F32),
            pltpu.VMEM((nb, nc, D_INNER, N_STATE), F32),
            pltpu.VMEM((nb, nc, 2, CHUNK, LANE), F32),
        ],
        compiler_params=_cparams("parallel", "arbitrary", "arbitrary"),
        name="ssd",
    )(*args)


def _post_kernel(oa_ref, y_ref, z_ref, oc_ref, gate_ref, x_ref, mod_ref, gb_ref, g2_ref,
                 wa_ref, wb_ref, wc_ref, wo_ref, xo_ref, h2_ref, *, n_sub):
    sub = x_ref.shape[0] // n_sub
    for r in range(n_sub):
        rows = slice(r * sub, (r + 1) * sub)
        yb = _rms(y_ref[rows, :] * _silu(z_ref[rows, :].astype(F32)), gb_ref[...]).astype(BF16)
        out_a = jnp.dot(oa_ref[rows, :], wa_ref[...], preferred_element_type=F32)
        out_b = jnp.dot(yb, wb_ref[...], preferred_element_type=F32)
        out_c = jnp.dot(oc_ref[rows, :], wc_ref[...], preferred_element_type=F32)
        merged = (jax.nn.sigmoid(gate_ref[rows, 0:D_MODEL].astype(F32)) * out_a
                  + jax.nn.sigmoid(gate_ref[rows, D_MODEL:2 * D_MODEL].astype(F32)) * out_b
                  + jax.nn.sigmoid(gate_ref[rows, 2 * D_MODEL:3 * D_MODEL].astype(F32)) * out_c)
        out = jnp.dot(merged.astype(BF16), wo_ref[...], preferred_element_type=F32)
        xn = x_ref[rows, :] + mod_ref[0, 2:3, :] * out
        xo_ref[rows, :] = xn
        h2 = _rms(xn, g2_ref[...]) * (1.0 + mod_ref[0, 4:5, :]) + mod_ref[0, 3:4, :]
        h2_ref[rows, :] = h2.astype(BF16)


POST_SUB = 2


def _post_call(oa, y, proj, oc, x2d, mod, gb, g2, wa, wb, wc, wo, row_fn, tm):
    t = x2d.shape[0]
    full = lambda i: (0, 0)
    return pl.pallas_call(
        functools.partial(_post_kernel, n_sub=POST_SUB),
        grid=(t // tm,),
        in_specs=[
            pl.BlockSpec((tm, H_A * V_A), lambda i: (i, 0)),
            pl.BlockSpec((tm, D_INNER), lambda i: (i, 0)),
            pl.BlockSpec((tm, D_INNER), lambda i: (i, C_Z // D_INNER)),
            pl.BlockSpec((tm, H_C * DV_C), lambda i: (i, 0)),
            pl.BlockSpec((tm, 3 * D_MODEL), lambda i: (i, 0)),
            pl.BlockSpec((tm, D_MODEL), lambda i: (i, 0)),
            pl.BlockSpec((1, N_MOD, D_MODEL), lambda i: (row_fn(i, tm), 0, 0)),
            pl.BlockSpec((1, D_INNER), full),
            pl.BlockSpec((1, D_MODEL), full),
            pl.BlockSpec((H_A * V_A, D_MODEL), full),
            pl.BlockSpec((D_INNER, D_MODEL), full),
            pl.BlockSpec((H_C * DV_C, D_MODEL), full),
            pl.BlockSpec((D_MODEL, D_MODEL), full),
        ],
        out_specs=[
            pl.BlockSpec((tm, D_MODEL), lambda i: (i, 0)),
            pl.BlockSpec((tm, D_MODEL), lambda i: (i, 0)),
        ],
        out_shape=[
            jax.ShapeDtypeStruct((t, D_MODEL), F32),
            jax.ShapeDtypeStruct((t, D_MODEL), BF16),
        ],
        compiler_params=_cparams("parallel"),
        name="post_mix",
    )(oa, y, proj, oc, proj, x2d, mod, gb, g2, wa, wb, wc, wo)


FFN_CHUNK = 256
FFN_DOWN_PARTS = 2


def _ffn_kernel(h_ref, prev_ref, next_ref, wu_ref, cw_ref, cb_ref, wd_ref, x_ref, mod_ref, o_ref, a_scr, *, tm, seq):
    i = pl.program_id(0)
    ext = tm + 2 * HALO
    if tm > seq:
        pos = (i * tm + lax.broadcasted_iota(jnp.int32, (tm, 1), 0)) % seq
        has_prev = pos != 0
        has_next = pos != seq - 1
        hext = jnp.concatenate([prev_ref[...], h_ref[...], next_ref[...]], axis=0)
    else:
        keep_prev = jnp.where((i * tm) % seq == 0, 0.0, 1.0).astype(BF16)
        keep_next = jnp.where(((i + 1) * tm) % seq == 0, 0.0, 1.0).astype(BF16)
        hext = jnp.concatenate([prev_ref[...] * keep_prev, h_ref[...], next_ref[...] * keep_next], axis=0)

    def conv(u, c0):
        up = pltpu.roll(u, 1, 0)[HALO:HALO + tm]
        un = pltpu.roll(u, ext - 1, 0)[HALO:HALO + tm]
        if tm > seq:
            up = jnp.where(has_prev, up, 0.0)
            un = jnp.where(has_next, un, 0.0)
        cols = slice(c0, c0 + FFN_CHUNK)
        return (cw_ref[0:1, cols] * up + cw_ref[1:2, cols] * u[HALO:HALO + tm]
                + cw_ref[2:3, cols] * un + cb_ref[:, cols])

    n_chunks = D_FF // FFN_CHUNK
    ends = [(n_chunks * (p + 1)) // FFN_DOWN_PARTS * FFN_CHUNK for p in range(FFN_DOWN_PARTS)]
    out = None
    start = 0
    for c in range(n_chunks):
        g0, v0 = c * FFN_CHUNK, D_FF + c * FFN_CHUNK
        ug = jnp.dot(hext, wu_ref[:, g0:g0 + FFN_CHUNK], preferred_element_type=F32)
        uv = jnp.dot(hext, wu_ref[:, v0:v0 + FFN_CHUNK], preferred_element_type=F32)
        a_scr[:, g0:g0 + FFN_CHUNK] = (_silu(conv(ug, g0)) * conv(uv, v0)).astype(BF16)
        end = g0 + FFN_CHUNK
        if end in ends:
            part = jnp.dot(a_scr[:, start:end], wd_ref[start:end, :], preferred_element_type=F32)
            out = part if out is None else out + part
            start = end
    o_ref[...] = x_ref[...] + mod_ref[0, 5:6, :] * out


def _ffn_call(h2, wu, cw, cb, wd, x2d, mod, row_fn, seq, tm):
    t = h2.shape[0]
    n = 2 * D_FF
    full = lambda i: (0, 0)
    return pl.pallas_call(
        functools.partial(_ffn_kernel, tm=tm, seq=seq),
        grid=(t // tm,),
        in_specs=_halo_specs(tm, D_MODEL, 0, t) + [
            pl.BlockSpec((D_MODEL, n), full),
            pl.BlockSpec((3, n), full),
            pl.BlockSpec((1, n), full),
            pl.BlockSpec((D_FF, D_MODEL), full),
            pl.BlockSpec((tm, D_MODEL), lambda i: (i, 0)),
            pl.BlockSpec((1, N_MOD, D_MODEL), lambda i: (row_fn(i, tm), 0, 0)),
        ],
        out_specs=pl.BlockSpec((tm, D_MODEL), lambda i: (i, 0)),
        out_shape=jax.ShapeDtypeStruct((t, D_MODEL), F32),
        scratch_shapes=[pltpu.VMEM((tm, D_FF), BF16)],
        compiler_params=_cparams("parallel"),
        name="ffn",
    )(h2, h2, h2, wu, cw, cb, wd, x2d, mod)


def _final_norm_kernel(x_ref, g_ref, o_ref):
    o_ref[...] = _rms(x_ref[...], g_ref[...])


def _final_norm_call(x2d, g, tm):
    t = x2d.shape[0]
    return pl.pallas_call(
        _final_norm_kernel,
        grid=(t // tm,),
        in_specs=[pl.BlockSpec((tm, D_MODEL), lambda i: (i, 0)), pl.BlockSpec((1, D_MODEL), lambda i: (0, 0))],
        out_specs=pl.BlockSpec((tm, D_MODEL), lambda i: (i, 0)),
        out_shape=jax.ShapeDtypeStruct((t, D_MODEL), F32),
        compiler_params=_cparams("parallel"),
        name="final_norm",
    )(x2d, g)


def _rope_tables(n_tokens, rot_dim, lead, tail):
    rows = n_tokens // GRID_W
    row = jnp.broadcast_to(jnp.arange(rows, dtype=F32)[:, None], (rows, GRID_W)).reshape(-1)
    col = jnp.broadcast_to(jnp.arange(GRID_W, dtype=F32)[None, :], (rows, GRID_W)).reshape(-1)
    n_freq = rot_dim // 4
    freqs = ROPE_THETA ** (-jnp.arange(n_freq, dtype=F32) / n_freq)
    ang = jnp.concatenate([row[:, None] * freqs, col[:, None] * freqs], axis=-1)
    cos, sin = jnp.cos(ang), jnp.sin(ang)
    c = jnp.repeat(cos, 2, axis=-1)
    s = jnp.stack([-sin, sin], axis=-1).reshape(n_tokens, rot_dim)
    reps = (LANE - lead - tail) // rot_dim
    c = jnp.tile(c, (1, reps))
    s = jnp.tile(s, (1, reps))
    c = jnp.concatenate([jnp.ones((n_tokens, lead), F32), c, jnp.ones((n_tokens, tail), F32)], axis=-1)
    s = jnp.concatenate([jnp.zeros((n_tokens, lead), F32), s, jnp.zeros((n_tokens, tail), F32)], axis=-1)
    return c, s


def _pad_w_in(w):
    cq, ckv, krope, z, xbc, dt, qc, kc, vc, gate = jnp.split(
        w, [384, 640, 672, 1696, 3744, 3776, 4288, 4800, 5312], axis=-1)
    d = w.shape[0]
    zeros = lambda n: jnp.zeros((d, n), w.dtype)
    kr_pad = jnp.concatenate([zeros(NOPE_A), krope, zeros(LANE - NOPE_A - ROPE_A)], axis=-1)
    dt_pad = jnp.concatenate([dt, zeros(LANE - 2 * H_B)], axis=-1)
    main = jnp.concatenate([gate, z, xbc, qc, kc, vc, cq, kr_pad, ckv], axis=-1)
    side_ctx = jnp.concatenate([kc, vc, ckv, kr_pad, dt_pad], axis=-1)
    return main.astype(BF16), side_ctx.astype(BF16), dt_pad.astype(BF16)


S_KC, S_VC, S_CKV, S_KROPE, S_DT = 0, 512, 1024, 1280, 1408


def _prep_layer(l, w):
    d_q = NOPE_A + ROPE_A
    w_uq = w['w_uq'][l].reshape(Q_RANK, H_A, d_q)
    w_uq = jnp.pad(w_uq, ((0, 0), (0, 0), (0, LANE - d_q))).reshape(Q_RANK, H_A * LANE)
    w_ukv = w['w_ukv'][l].reshape(KV_RANK, H_A, NOPE_A + V_A)
    wk = jnp.pad(w_ukv[:, :, :NOPE_A], ((0, 0), (0, 0), (0, LANE - NOPE_A))).reshape(KV_RANK, H_A * LANE)
    wv = w_ukv[:, :, NOPE_A:].reshape(KV_RANK, H_A // 2, 2, V_A)
    zv = jnp.zeros_like(wv[:, :, 0])
    wv = jnp.stack([jnp.concatenate([wv[:, :, 0], zv], -1), jnp.concatenate([zv, wv[:, :, 1]], -1)],
                   axis=2).reshape(KV_RANK, H_A * LANE)
    pad_lane = lambda v: jnp.pad(v.reshape(1, -1), ((0, 0), (0, LANE - v.size)))
    w_in, w_side_ctx, w_side_lat = _pad_w_in(w['w_in'][l])
    return dict(
        w_in=w_in, w_side_ctx=w_side_ctx, w_side_lat=w_side_lat,
        norm1_g=w['norm1_g'][l].reshape(1, -1),
        norm2_g=w['norm2_g'][l].reshape(1, -1),
        q_norm_g=w['q_norm_g'][l].reshape(1, -1),
        kv_norm_g=w['kv_norm_g'][l].reshape(1, -1),
        w_uq=w_uq.astype(BF16),
        w_ukv=jnp.concatenate([wk, wv], axis=-1).astype(BF16),
        w_oA=w['w_oA'][l].astype(BF16),
        w_oB=w['w_oB'][l].astype(BF16),
        w_oC=w['w_oC'][l].astype(BF16),
        w_out=w['w_out'][l].astype(BF16),
        ssd_conv_w=w['ssd_conv_w'][l],
        ssd_conv_b=w['ssd_conv_b'][l].reshape(1, -1),
        dt_bias=pad_lane(w['ssd_dt_bias'][l]),
        a_log=pad_lane(w['ssd_a_log'][l]),
        d_skip=jnp.repeat(w['ssd_d_skip'][l], P_B).reshape(1, -1),
        ssd_norm_g=w['ssd_norm_g'][l].reshape(1, -1),
        diff_lambda=w['diff_lambda'][l],
        diff_norm_g=w['diff_norm_g'][l].reshape(1, -1),
        ffn_w_up=w['ffn_w_up'][l].astype(BF16),
        ffn_conv_w=w['ffn_conv_w'][l],
        ffn_conv_b=w['ffn_conv_b'][l].reshape(1, -1),
        ffn_w_down=w['ffn_w_down'][l].astype(BF16),
    )


def _tile(n, pref):
    t = min(n, pref)
    while n % t:
        t //= 2
    return t


def _block(x, mod, lp, lam_init, row_fn, tabs, ctx):
    b, seq, _ = x.shape
    t = b * seq
    x2d = x.reshape(t, D_MODEL)
    is_ctx = ctx is None
    tm_big = _tile(t if is_ctx else seq, 1024)
    tm = _tile(seq, 256)
    proj, side = _inproj_call(x2d, lp['norm1_g'], mod, lp['w_in'],
                              lp['w_side_ctx'] if is_ctx else lp['w_side_lat'], row_fn, tm_big)
    proj3 = proj.reshape(b, seq, N_PROJ)
    tab_a, tab_c = tabs if tabs is not None else (None, None)
    qa = _mla_q_call(proj, lp['q_norm_g'], lp['w_uq'], tab_a, seq, tm_big)
    if is_ctx:
        k_new, v_new, ckv_n = _kv_up_call(side, S_CKV // KV_RANK, side, S_KROPE // LANE, lp['kv_norm_g'],
                                          lp['w_ukv'], None, seq, tm_big, want_ckv=True)
    else:
        k_new, v_new = _kv_up_call(proj, C_CKV // KV_RANK, proj, C_KROPE // LANE, lp['kv_norm_g'], lp['w_ukv'],
                                   tab_a, seq, tm_big)
    wide = H_A * LANE
    pieces_a = [(k_new.reshape(b, seq, wide), 0, v_new.reshape(b, seq, wide), 0)]
    qd, kd = _dprep_call(proj, tab_c, seq, tm_big)
    pieces_c = [(kd.reshape(b, seq, H_C * LANE), 0, proj3, C_VC // LANE)]
    h0 = None
    if not is_ctx:
        ctx_ckv, ctx_krope, ctx_k, ctx_v, h0 = ctx
        past = ctx_ckv.shape[1]
        kr_pad = jnp.pad(ctx_krope.reshape(b * past, ROPE_A), ((0, 0), (NOPE_A, LANE - NOPE_A - ROPE_A)))
        k_c, v_c = _kv_up_call(ctx_ckv.reshape(b * past, KV_RANK), 0, kr_pad, 0, None, lp['w_ukv'],
                               None, past, _tile(b * past, 1024))
        pieces_a = [(k_c.reshape(b, past, wide), 0, v_c.reshape(b, past, wide), 0)] + pieces_a
        pieces_c = [(ctx_k.reshape(b, past, H_C * LANE), 0, ctx_v.reshape(b, past, H_C * DV_C), 0)] + pieces_c
        h0 = h0.reshape(b, 2, D_INNER, N_STATE)
    oa = _attn_call(qa.reshape(b, seq, wide), pieces_a, H_A, False, _tile(seq, 1024), H_A if is_ctx else 2)
    oc = _attn_call(qd.reshape(b, seq, H_C * LANE), pieces_c, H_C, True, _tile(seq, 512), H_C,
                    extra=(lp['diff_lambda'], lp['diff_norm_g']), lam_init=lam_init)
    xbc = _ssd_conv_call(proj, lp['ssd_conv_w'], lp['ssd_conv_b'], seq, tm)
    y, hfin = _ssd_call(xbc.reshape(b, seq, CONV_CH), side.reshape(b, seq, side.shape[1]),
                        S_DT // LANE if is_ctx else 0, lp['dt_bias'], lp['a_log'], lp['d_skip'], h0)
    x2d, h2 = _post_call(oa.reshape(t, H_A * V_A), y.reshape(t, D_INNER), proj, oc.reshape(t, H_C * DV_C), x2d, mod,
                         lp['ssd_norm_g'], lp['norm2_g'], lp['w_oA'], lp['w_oB'], lp['w_oC'], lp['w_out'],
                         row_fn, _tile(t if is_ctx else seq, 512))
    x2d = _ffn_call(h2, lp['ffn_w_up'], lp['ffn_conv_w'], lp['ffn_conv_b'], lp['ffn_w_down'], x2d, mod, row_fn,
                    seq, _tile(t if is_ctx else seq, 1024))
    st = None
    if is_ctx:
        st = (ckv_n.reshape(b, seq, KV_RANK),
              side[:, S_KROPE + NOPE_A:S_KROPE + NOPE_A + ROPE_A].reshape(b, seq, ROPE_A),
              side[:, S_KC:S_KC + H_C * 2 * DH_C].reshape(b, seq, H_C, 2, DH_C),
              side[:, S_VC:S_VC + H_C * DV_C].reshape(b, seq, H_C, DV_C),
              hfin.reshape(b, 2, H_B, P_B, N_STATE))
    return x2d.reshape(b, seq, D_MODEL), st


def kernel(x_prompt, x_sample, cache_mla_ckv, cache_mla_krope, cache_diff_k, cache_diff_v, state_ssd, c, c_ctx, w_mod, b_mod, norm1_g, norm2_g, w_in, q_norm_g, kv_norm_g, w_uq, w_ukv, w_oA, ssd_conv_w, ssd_conv_b, ssd_dt_bias, ssd_a_log, ssd_d_skip, ssd_norm_g, w_oB, diff_lambda, diff_norm_g, w_oC, w_out, ffn_w_up, ffn_conv_w, ffn_conv_b, ffn_w_down, final_norm_g):
    weights = dict(norm1_g=norm1_g, norm2_g=norm2_g, w_in=w_in, q_norm_g=q_norm_g, kv_norm_g=kv_norm_g,
                   w_uq=w_uq, w_ukv=w_ukv, w_oA=w_oA, ssd_conv_w=ssd_conv_w, ssd_conv_b=ssd_conv_b,
                   ssd_dt_bias=ssd_dt_bias, ssd_a_log=ssd_a_log, ssd_d_skip=ssd_d_skip, ssd_norm_g=ssd_norm_g,
                   w_oB=w_oB, diff_lambda=diff_lambda, diff_norm_g=diff_norm_g, w_oC=w_oC, w_out=w_out,
                   ffn_w_up=ffn_w_up, ffn_conv_w=ffn_conv_w, ffn_conv_b=ffn_conv_b, ffn_w_down=ffn_w_down)
    n_dec, n_lat = x_sample.shape[0], x_sample.shape[1]
    ctx_row = n_dec
    cond = jnp.concatenate([c, c_ctx[None, :], jnp.zeros((COND_ROWS - n_dec - 1, D_MODEL), F32)], axis=0)
    mod_all = _mod_call(cond, w_mod, b_mod).reshape(DEPTH, COND_ROWS, N_MOD, D_MODEL)
    tabs = (_rope_tables(n_lat, ROPE_A, NOPE_A, LANE - NOPE_A - ROPE_A), _rope_tables(n_lat, DH_C, 0, 0))
    lat_row = lambda i, tm: (i * tm) // n_lat
    ctx_row_fn = lambda i, tm: ctx_row
    xp, xs = x_prompt, x_sample
    new = [[] for _ in range(5)]
    for l in range(DEPTH):
        lp = _prep_layer(l, weights)
        lam_init = 0.8 - 0.6 * math.exp(-0.3 * l)
        xp, st = _block(xp, mod_all[l], lp, lam_init, ctx_row_fn, None, None)
        for acc, v in zip(new, st):
            acc.append(v)
        ctx = (cache_mla_ckv[:, l], cache_mla_krope[:, l], cache_diff_k[:, l], cache_diff_v[:, l], state_ssd[:, l])
        xs, _ = _block(xs, mod_all[l], lp, lam_init, lat_row, tabs, ctx)
    bp, sp = xp.shape[0], xp.shape[1]
    y_prompt = _final_norm_call(xp.reshape(bp * sp, D_MODEL), final_norm_g.reshape(1, -1), _tile(bp * sp, 1024))
    y_sample = _final_norm_call(xs.reshape(n_dec * n_lat, D_MODEL), final_norm_g.reshape(1, -1),
                                _tile(n_dec * n_lat, 1024))
    return (y_prompt.reshape(bp, sp, D_MODEL), y_sample.reshape(n_dec, n_lat, D_MODEL),
            jnp.stack(new[0], axis=1), jnp.stack(new[1], axis=1), jnp.stack(new[2], axis=1),
            jnp.stack(new[3], axis=1), jnp.stack(new[4], axis=1))
```
